```python
import jax, jax.numpy as jnp
from jax import lax
import numpy as np

D_MODEL = 4096
BATCH = 4
SEQ = 2048
DEPTH = 1
DEC_BATCH = 128
DEC_SEQ = 1
PAST_LEN = 16384
PAGE_SIZE = 128

HEAD_SIZE = 64
D_A = D_MODEL // 2
N_HEADS_A = D_A // HEAD_SIZE
DECAY_LORA = 96
AAA_LORA = 96
GATE_LORA = 256
A_PROJ = 3 * D_A + DECAY_LORA + AAA_LORA + GATE_LORA
D_B = D_MODEL // 2
POOL_WINDOWS = (2, 4, 8, 16)
N_POOL_GROUPS = len(POOL_WINDOWS)
POOL_GROUP = D_B // N_POOL_GROUPS
POOL_HIST = max(POOL_WINDOWS) - 1
IN_PROJ = A_PROJ + D_B + 2 * D_MODEL
D_FF = 3 * D_MODEL
CONV_W = 3
PLE_DIM = 256
EPS = 1e-6
GN_EPS = 64e-5

kernel_name = 'rwkv7_pool_gated_hybrid_step'


def _rmsnorm(x, g):
    xf = x.astype(jnp.float32)
    y = xf * lax.rsqrt(jnp.mean(xf * xf, axis=-1, keepdims=True) + EPS)
    return (y * g.astype(jnp.float32)).astype(x.dtype)


def _token_shift(y, hist, mu):
    prev = jnp.concatenate([hist.astype(y.dtype), y[:, :-1]], axis=1)
    return y + mu * (prev - y), y[:, -1:]


def _wkv7_scan(s0, r, decay, k, v, a_vec, b_vec):
    def step(S, inp):
        r_t, w_t, k_t, v_t, a_t, b_t = inp
        sa = jnp.einsum('bhvk,bhk->bhv', S, a_t)
        S = S * w_t[:, :, None, :] + sa[..., None] * b_t[:, :, None, :] + v_t[..., None] * k_t[:, :, None, :]
        y_t = jnp.einsum('bhvk,bhk->bhv', S, r_t)
        return S, y_t
    xs = tuple(jnp.moveaxis(u, 1, 0) for u in (r, decay, k, v, a_vec, b_vec))
    s_new, ys = lax.scan(step, s0, xs)
    return jnp.moveaxis(ys, 0, 1), s_new


def _rwkv7(za, s0, w0, w_decay_up, a0, w_aaa_up, w_gate_up, k_k, k_a, r_k, ln_x_g, ln_x_b):
    b, t, _ = za.shape
    zf = za.astype(jnp.float32)
    o1, o2, o3 = D_A, 2 * D_A, 3 * D_A
    o4 = o3 + DECAY_LORA
    o5 = o4 + AAA_LORA
    r, k, v = zf[..., :o1], zf[..., o1:o2], zf[..., o2:o3]
    xw, xa, xg = zf[..., o3:o4], zf[..., o4:o5], zf[..., o5:]
    w = -jax.nn.softplus(-(w0 + jnp.tanh(xw) @ w_decay_up)) - 0.5
    decay = jnp.exp(-jnp.exp(w))
    a = jax.nn.sigmoid(a0 + xa @ w_aaa_up)
    g = jax.nn.sigmoid(xg) @ w_gate_up

    def heads(u):
        return u.reshape(b, t, N_HEADS_A, HEAD_SIZE)
    kk = heads(k * k_k)
    kk = kk / jnp.maximum(jnp.sqrt(jnp.sum(kk * kk, axis=-1, keepdims=True)), 1e-12)
    k = k * (1.0 + (a - 1.0) * k_a)
    rh, kh, vh, ah = heads(r), heads(k), heads(v), heads(a)
    y, s_new = _wkv7_scan(s0.astype(jnp.float32), rh, heads(decay), kh, vh, -kk, kk * ah)
    mu = jnp.mean(y, axis=-1, keepdims=True)
    var = jnp.mean(jnp.square(y - mu), axis=-1, keepdims=True)
    y = ((y - mu) * lax.rsqrt(var + GN_EPS)).reshape(b, t, D_A) * ln_x_g + ln_x_b
    bonus = jnp.sum(rh * kh * r_k, axis=-1, keepdims=True) * vh
    o = (y + bonus.reshape(b, t, D_A)) * g
    return o.astype(za.dtype), s_new


def _pool_mix(u, hist, pos0, w_pool, pool_scale):
    b, t, _ = u.shape
    ext = jnp.concatenate([hist.astype(u.dtype), u], axis=1)
    cs = jnp.cumsum(ext.astype(jnp.float32), axis=1)
    cs = jnp.pad(cs, ((0, 0), (1, 0), (0, 0)))
    pos = pos0 + jnp.arange(t)
    means = []
    for gi, win in enumerate(POOL_WINDOWS):
        c0, c1 = gi * POOL_GROUP, (gi + 1) * POOL_GROUP
        hi = cs[:, POOL_HIST + 1:POOL_HIST + 1 + t, c0:c1]
        lo = cs[:, POOL_HIST + 1 - win:POOL_HIST + 1 - win + t, c0:c1]
        cnt = jnp.minimum(pos + 1, win).astype(jnp.float32)[None, :, None]
        means.append((hi - lo) / cnt)
    pooled = jnp.concatenate(means, axis=-1) - u.astype(jnp.float32)
    pooled = pooled.reshape(b, t, N_POOL_GROUPS, POOL_GROUP).astype(u.dtype)
    mixed = jnp.einsum('btgc,gcd->btgd', pooled, w_pool).reshape(b, t, D_B) * pool_scale
    return mixed, ext[:, -POOL_HIST:]


def _conv_ffn(xn, hist, w_ffn_gate, w_ffn_up, conv_k, conv_b, w_ffn_down):
    t = xn.shape[1]
    hg = xn @ w_ffn_gate
    ext = jnp.concatenate([hist.astype(hg.dtype), hg], axis=1)
    c = conv_b + sum(ext[:, j:j + t] * conv_k[j] for j in range(CONV_W))
    h = jax.nn.gelu(c) * (xn @ w_ffn_up)
    return h @ w_ffn_down, ext[:, -(CONV_W - 1):]


def _layer(h, p, s_wkv, s_shift, s_pool, s_conv, pos0,
           norm_mix, w_in, mu_shift, w0, w_decay_up, a0, w_aaa_up, w_gate_up, k_k, k_a, r_k,
           ln_x_g, ln_x_b, w_o_a, w_pool, pool_scale, w_o_b, w_out, norm_ffn, w_ffn_gate, w_ffn_up,
           conv_k, conv_b, w_ffn_down, norm_ple, w_ple_gate, w_ple):
    xn = _rmsnorm(h, norm_mix)
    proj = xn @ w_in
    za, new_shift = _token_shift(proj[..., :A_PROJ], s_shift, mu_shift)
    o_a, new_wkv = _rwkv7(za, s_wkv, w0, w_decay_up, a0, w_aaa_up, w_gate_up, k_k, k_a, r_k, ln_x_g, ln_x_b)
    u = proj[..., A_PROJ:A_PROJ + D_B]
    o_b, new_pool = _pool_mix(u, s_pool, pos0, w_pool, pool_scale)
    g0 = A_PROJ + D_B
    gate_a = jax.nn.sigmoid(proj[..., g0:g0 + D_MODEL])
    gate_b = jax.nn.sigmoid(proj[..., g0 + D_MODEL:])
    merged = gate_a * (o_a @ w_o_a) + gate_b * (o_b @ w_o_b)
    h = h + merged @ w_out
    f, new_conv = _conv_ffn(_rmsnorm(h, norm_ffn), s_conv, w_ffn_gate, w_ffn_up, conv_k, conv_b, w_ffn_down)
    h = h + f
    gate_p = jax.nn.sigmoid(_rmsnorm(h, norm_ple) @ w_ple_gate)
    h = h + gate_p * (p @ w_ple)
    return h, (new_wkv, new_shift, new_pool, new_conv)


def setup_inputs(seed: int = 0) -> dict:
    key = jax.random.key(seed)
    ks = jax.random.split(key, 40)
    f32 = jnp.float32
    L = DEPTH

    def nrm(k, shape, s):
        return jax.random.normal(k, shape, f32) * s
    return {
        'x_prompt': nrm(ks[0], (BATCH, SEQ, D_MODEL), 1.0),
        'x_sample': nrm(ks[1], (DEC_BATCH, DEC_SEQ, D_MODEL), 1.0),
        'p_prompt': nrm(ks[2], (L, BATCH, SEQ, PLE_DIM), 1.0),
        'p_sample': nrm(ks[3], (L, DEC_BATCH, DEC_SEQ, PLE_DIM), 1.0),
        'state_wkv': nrm(ks[4], (L, DEC_BATCH, N_HEADS_A, HEAD_SIZE, HEAD_SIZE), 0.3),
        'state_shift': nrm(ks[5], (L, DEC_BATCH, 1, A_PROJ), 1.0),
        'state_pool': nrm(ks[6], (L, DEC_BATCH, POOL_HIST, D_B), 1.0),
        'state_conv': nrm(ks[7], (L, DEC_BATCH, CONV_W - 1, D_FF), 1.0),
        'norm_mix': 1.0 + nrm(ks[8], (L, D_MODEL), 0.05),
        'w_in': nrm(ks[9], (L, D_MODEL, IN_PROJ), D_MODEL ** -0.5),
        'mu_shift': jax.random.uniform(ks[10], (L, A_PROJ), f32),
        'w0': -1.0 + nrm(ks[11], (L, D_A), 0.5),
        'w_decay_up': nrm(ks[12], (L, DECAY_LORA, D_A), 0.5 * DECAY_LORA ** -0.5),
        'a0': nrm(ks[13], (L, D_A), 0.5),
        'w_aaa_up': nrm(ks[14], (L, AAA_LORA, D_A), 0.5 * AAA_LORA ** -0.5),
        'w_gate_up': nrm(ks[15], (L, GATE_LORA, D_A), GATE_LORA ** -0.5),
        'k_k': 1.0 + nrm(ks[16], (L, D_A), 0.1),
        'k_a': 1.0 + nrm(ks[17], (L, D_A), 0.1),
        'r_k': nrm(ks[18], (L, N_HEADS_A, HEAD_SIZE), 0.1),
        'ln_x_g': 1.0 + nrm(ks[19], (L, D_A), 0.05),
        'ln_x_b': nrm(ks[20], (L, D_A), 0.02),
        'w_o_a': nrm(ks[21], (L, D_A, D_MODEL), D_A ** -0.5),
        'w_pool': nrm(ks[22], (L, N_POOL_GROUPS, POOL_GROUP, POOL_GROUP), POOL_GROUP ** -0.5),
        'pool_scale': 1.0 + nrm(ks[23], (L, D_B), 0.1),
        'w_o_b': nrm(ks[24], (L, D_B, D_MODEL), D_B ** -0.5),
        'w_out': nrm(ks[25], (L, D_MODEL, D_MODEL), D_MODEL ** -0.5),
        'norm_ffn': 1.0 + nrm(ks[26], (L, D_MODEL), 0.05),
        'w_ffn_gate': nrm(ks[27], (L, D_MODEL, D_FF), D_MODEL ** -0.5),
        'w_ffn_up': nrm(ks[28], (L, D_MODEL, D_FF), D_MODEL ** -0.5),
        'conv_k': nrm(ks[29], (L, CONV_W, D_FF), CONV_W ** -0.5),
        'conv_b': nrm(ks[30], (L, D_FF), 0.02),
        'w_ffn_down': nrm(ks[31], (L, D_FF, D_MODEL), D_FF ** -0.5),
        'norm_ple': 1.0 + nrm(ks[32], (L, D_MODEL), 0.05),
        'w_ple_gate': nrm(ks[33], (L, D_MODEL, D_MODEL), D_MODEL ** -0.5),
        'w_ple': nrm(ks[34], (L, PLE_DIM, D_MODEL), PLE_DIM ** -0.5),
        'norm_final': 1.0 + nrm(ks[35], (D_MODEL,), 0.05),
    }


def reference(x_prompt, x_sample, p_prompt, p_sample, state_wkv, state_shift, state_pool, state_conv,
              norm_mix, w_in, mu_shift, w0, w_decay_up, a0, w_aaa_up, w_gate_up, k_k, k_a, r_k,
              ln_x_g, ln_x_b, w_o_a, w_pool, pool_scale, w_o_b, w_out, norm_ffn, w_ffn_gate, w_ffn_up,
              conv_k, conv_b, w_ffn_down, norm_ple, w_ple_gate, w_ple, norm_final):
    bp = x_prompt.shape[0]
    dt = x_prompt.dtype
    hp, hs = x_prompt, x_sample
    outs_p = ([], [], [], [])
    outs_s = ([], [], [], [])
    for i in range(DEPTH):
        lw = [arr[i] for arr in (norm_mix, w_in, mu_shift, w0, w_decay_up, a0, w_aaa_up, w_gate_up, k_k, k_a,
                                 r_k, ln_x_g, ln_x_b, w_o_a, w_pool, pool_scale, w_o_b, w_out, norm_ffn,
                                 w_ffn_gate, w_ffn_up, conv_k, conv_b, w_ffn_down, norm_ple, w_ple_gate, w_ple)]
        z_wkv = jnp.zeros((bp, N_HEADS_A, HEAD_SIZE, HEAD_SIZE), jnp.float32)
        z_shift = jnp.zeros((bp, 1, A_PROJ), dt)
        z_pool = jnp.zeros((bp, POOL_HIST, D_B), dt)
        z_conv = jnp.zeros((bp, CONV_W - 1, D_FF), dt)
        hp, st_p = _layer(hp, p_prompt[i], z_wkv, z_shift, z_pool, z_conv, 0, *lw)
        hs, st_s = _layer(hs, p_sample[i], state_wkv[i], state_shift[i], state_pool[i], state_conv[i],
                          PAST_LEN, *lw)
        for lst, s in zip(outs_p, st_p):
            lst.append(s)
        for lst, s in zip(outs_s, st_s):
            lst.append(s)
    y_prompt = _rmsnorm(hp, norm_final)
    y_sample = _rmsnorm(hs, norm_final)
    return (y_prompt, y_sample,
            jnp.stack(outs_p[0]), jnp.stack(outs_p[1]), jnp.stack(outs_p[2]), jnp.stack(outs_p[3]),
            jnp.stack(outs_s[0]), jnp.stack(outs_s[1]), jnp.stack(outs_s[2]), jnp.stack(outs_s[3]))
```

```python
import functools

import jax
import jax.numpy as jnp
from jax import lax
from jax.experimental import pallas as pl
from jax.experimental.pallas import tpu as pltpu

_EPS = 1e-6
_GN_EPS = 64e-5
_POOL_WINDOWS = (2, 4, 8, 16)
_PAST_LEN = 16384
_KK_FLOOR = 1e-12

_LANE = 128
_SUBLANE = 8
_MXU_DIM = 256
_VMEM_LIMIT_BYTES = 56 * 1024 * 1024

_F32 = jnp.float32
_BF16 = jnp.bfloat16


def _cparams(n_grid):
    return pltpu.CompilerParams(dimension_semantics=("arbitrary",) * n_grid,
                                vmem_limit_bytes=_VMEM_LIMIT_BYTES)


def _block(n, target):
    if n <= target:
        return n
    b = target
    while b >= _SUBLANE:
        if n % b == 0 and b % _SUBLANE == 0:
            return b
        b -= _SUBLANE
    return n


def _sigmoid(x):
    return 1.0 / (1.0 + jnp.exp(-x))


def _softplus(x):
    return jnp.maximum(x, 0.0) + jnp.log1p(jnp.exp(-jnp.abs(x)))


def _gelu_tanh(x):
    return x * (0.5 * (1.0 + jnp.tanh(0.7978845608028654 * (x + 0.044715 * (x * x * x)))))


def _dot(a, b):
    return jnp.dot(a, b, preferred_element_type=_F32)


def _rmsnorm_kernel(x_ref, g_ref, o_ref):
    x = x_ref[...]
    y = x * lax.rsqrt(jnp.mean(x * x, axis=-1, keepdims=True) + _EPS)
    o_ref[...] = (y * g_ref[...]).astype(o_ref.dtype)


def _rmsnorm(x, g, out_dtype):
    m, d = x.shape
    bm = _block(m, 256)
    return pl.pallas_call(
        _rmsnorm_kernel,
        grid=(m // bm,),
        in_specs=[pl.BlockSpec((bm, d), lambda i: (i, 0)), pl.BlockSpec((1, d), lambda i: (0, 0))],
        out_specs=pl.BlockSpec((bm, d), lambda i: (i, 0)),
        out_shape=jax.ShapeDtypeStruct((m, d), out_dtype),
        compiler_params=_cparams(1),
        name="rmsnorm",
    )(x, g.reshape(1, d))


def _mm_kernel(x_ref, w_ref, o_ref, *, act):
    acc = _dot(x_ref[...], w_ref[...])
    if act == "sigmoid":
        acc = _sigmoid(acc)
    o_ref[...] = acc.astype(o_ref.dtype)


def _mm(x, w, *, act=None, out_dtype=_F32, bn_target=1024, name="mm"):
    m, k = x.shape
    n = w.shape[1]
    bm, bn = _block(m, 1024), _block(n, bn_target)
    return pl.pallas_call(
        functools.partial(_mm_kernel, act=act),
        grid=(m // bm, n // bn),
        in_specs=[pl.BlockSpec((bm, k), lambda i, j: (i, 0)), pl.BlockSpec((k, bn), lambda i, j: (0, j))],
        out_specs=pl.BlockSpec((bm, bn), lambda i, j: (i, j)),
        out_shape=jax.ShapeDtypeStruct((m, n), out_dtype),
        compiler_params=_cparams(2),
        name=name,
    )(x, w)


def _merge_kernel(oa_ref, ob_ref, wa_ref, wb_ref, ga_ref, gb_ref, o_ref):
    a = _dot(oa_ref[...], wa_ref[...])
    b = _dot(ob_ref[...], wb_ref[...])
    o_ref[...] = (ga_ref[...] * a + gb_ref[...] * b).astype(o_ref.dtype)


def _merge(oa, ob, wa, wb, gates):
    m, ka = oa.shape
    kb = ob.shape[1]
    n = wa.shape[1]
    bm, bn = _block(m, 1024), _block(n, 512)
    nj = n // bn
    return pl.pallas_call(
        _merge_kernel,
        grid=(m // bm, nj),
        in_specs=[
            pl.BlockSpec((bm, ka), lambda i, j: (i, 0)),
            pl.BlockSpec((bm, kb), lambda i, j: (i, 0)),
            pl.BlockSpec((ka, bn), lambda i, j: (0, j)),
            pl.BlockSpec((kb, bn), lambda i, j: (0, j)),
            pl.BlockSpec((bm, bn), lambda i, j: (i, j)),
            pl.BlockSpec((bm, bn), lambda i, j: (i, j + nj)),
        ],
        out_specs=pl.BlockSpec((bm, bn), lambda i, j: (i, j)),
        out_shape=jax.ShapeDtypeStruct((m, n), _BF16),
        compiler_params=_cparams(2),
        name="merge",
    )(oa, ob, wa, wb, gates, gates)


def _mm_res_kernel(x_ref, w_ref, r_ref, o_ref):
    o_ref[...] = r_ref[...] + _dot(x_ref[...], w_ref[...])


def _mm_res_ktiled_kernel(x_ref, w_ref, r_ref, o_ref, acc_ref):
    kk = pl.program_id(2)

    @pl.when(kk == 0)
    def _():
        acc_ref[...] = r_ref[...]

    acc_ref[...] += _dot(x_ref[...], w_ref[...])

    @pl.when(kk == pl.num_programs(2) - 1)
    def _():
        o_ref[...] = acc_ref[...]


def _mm_res(x, w, res, name):
    m, k = x.shape
    n = w.shape[1]
    bm, bn, bk = _block(m, 1024), _block(n, 512), _block(k, 4096)
    if bk == k:
        return pl.pallas_call(
            _mm_res_kernel,
            grid=(m // bm, n // bn),
            in_specs=[
                pl.BlockSpec((bm, k), lambda i, j: (i, 0)),
                pl.BlockSpec((k, bn), lambda i, j: (0, j)),
                pl.BlockSpec((bm, bn), lambda i, j: (i, j)),
            ],
            out_specs=pl.BlockSpec((bm, bn), lambda i, j: (i, j)),
            out_shape=jax.ShapeDtypeStruct((m, n), _F32),
            compiler_params=_cparams(2),
            name=name,
        )(x, w, res)
    return pl.pallas_call(
        _mm_res_ktiled_kernel,
        grid=(m // bm, n // bn, k // bk),
        in_specs=[
            pl.BlockSpec((bm, bk), lambda i, j, q: (i, q)),
            pl.BlockSpec((bk, bn), lambda i, j, q: (q, j)),
            pl.BlockSpec((bm, bn), lambda i, j, q: (i, j)),
        ],
        out_specs=pl.BlockSpec((bm, bn), lambda i, j, q: (i, j)),
        out_shape=jax.ShapeDtypeStruct((m, n), _F32),
        scratch_shapes=[pltpu.VMEM((bm, bn), _F32)],
        compiler_params=_cparams(3),
        name=name,
    )(x, w, res)


def _ple_kernel(x_ref, wg_ref, p_ref, wp_ref, r_ref, o_ref):
    gate = _sigmoid(_dot(x_ref[...], wg_ref[...]))
    o_ref[...] = r_ref[...] + gate * _dot(p_ref[...], wp_ref[...])


def _ple(xn, wg, p, wp, res):
    m, k = xn.shape
    kp = p.shape[1]
    n = wg.shape[1]
    bm, bn = _block(m, 1024), _block(n, 512)
    return pl.pallas_call(
        _ple_kernel,
        grid=(m // bm, n // bn),
        in_specs=[
            pl.BlockSpec((bm, k), lambda i, j: (i, 0)),
            pl.BlockSpec((k, bn), lambda i, j: (0, j)),
            pl.BlockSpec((bm, kp), lambda i, j: (i, 0)),
            pl.BlockSpec((kp, bn), lambda i, j: (0, j)),
            pl.BlockSpec((bm, bn), lambda i, j: (i, j)),
        ],
        out_specs=pl.BlockSpec((bm, bn), lambda i, j: (i, j)),
        out_shape=jax.ShapeDtypeStruct((m, n), _F32),
        compiler_params=_cparams(2),
        name="ple",
    )(xn, wg, p, wp, res)


def _ffn_up_seq_kernel(x_ref, wg_ref, wu_ref, ck_ref, cb_ref, h_ref, tail_ref, carry_ref, *, blocks_per_seq):
    i, j = pl.program_id(0), pl.program_id(1)
    x = x_ref[...]
    hg = _dot(x, wg_ref[...])
    up = _dot(x, wu_ref[...])
    bm = hg.shape[0]

    @pl.when(i % blocks_per_seq == 0)
    def _():
        carry_ref[j] = jnp.zeros(carry_ref.shape[1:], _F32)

    hist = carry_ref[j]
    row = lax.broadcasted_iota(jnp.int32, hg.shape, 0)
    prev1 = jnp.where(row == 0, hist[7:8], pltpu.roll(hg, 1, 0))
    prev2 = jnp.where(row == 0, hist[6:7], jnp.where(row == 1, hist[7:8], pltpu.roll(hg, 2, 0)))
    ck = ck_ref[...]
    c = cb_ref[...] + (prev2 * ck[0:1] + prev1 * ck[1:2] + hg * ck[2:3])
    h_ref[...] = (_gelu_tanh(c) * up).astype(h_ref.dtype)
    tail = hg[bm - _SUBLANE:, :]
    tail_ref[0] = tail
    carry_ref[j] = tail


def _ffn_up_seq(xn, wg, wu, conv_k8, conv_b, seq_len):
    m, k = xn.shape
    f = wg.shape[1]
    bm, bn = _block(seq_len, 1024), _block(f, 512)
    nj = f // bn
    return pl.pallas_call(
        functools.partial(_ffn_up_seq_kernel, blocks_per_seq=seq_len // bm),
        grid=(m // bm, nj),
        in_specs=[
            pl.BlockSpec((bm, k), lambda i, j: (i, 0)),
            pl.BlockSpec((k, bn), lambda i, j: (0, j)),
            pl.BlockSpec((k, bn), lambda i, j: (0, j)),
            pl.BlockSpec((_SUBLANE, bn), lambda i, j: (0, j)),
            pl.BlockSpec((1, bn), lambda i, j: (0, j)),
        ],
        out_specs=[
            pl.BlockSpec((bm, bn), lambda i, j: (i, j)),
            pl.BlockSpec((1, _SUBLANE, bn), lambda i, j: (i, 0, j)),
        ],
        out_shape=[
            jax.ShapeDtypeStruct((m, f), _BF16),
            jax.ShapeDtypeStruct((m // bm, _SUBLANE, f), _F32),
        ],
        scratch_shapes=[pltpu.VMEM((nj, _SUBLANE, bn), _F32)],
        compiler_params=_cparams(2),
        name="ffn_up_seq",
    )(xn, wg, wu, conv_k8, conv_b)


def _ffn_up_step_kernel(x_ref, wg_ref, wu_ref, ck_ref, cb_ref, h0_ref, h1_ref, h_ref, hg_ref):
    x = x_ref[...]
    hg = _dot(x, wg_ref[...])
    up = _dot(x, wu_ref[...])
    ck = ck_ref[...]
    c = cb_ref[...] + (h0_ref[...] * ck[0:1] + h1_ref[...] * ck[1:2] + hg * ck[2:3])
    h_ref[...] = (_gelu_tanh(c) * up).astype(h_ref.dtype)
    hg_ref[...] = hg


def _ffn_up_step(xn, wg, wu, conv_k8, conv_b, hist0, hist1):
    m, k = xn.shape
    f = wg.shape[1]
    bm, bn = _block(m, 1024), _block(f, 512)
    tile = pl.BlockSpec((bm, bn), lambda i, j: (i, j))
    return pl.pallas_call(
        _ffn_up_step_kernel,
        grid=(m // bm, f // bn),
        in_specs=[
            pl.BlockSpec((bm, k), lambda i, j: (i, 0)),
            pl.BlockSpec((k, bn), lambda i, j: (0, j)),
            pl.BlockSpec((k, bn), lambda i, j: (0, j)),
            pl.BlockSpec((_SUBLANE, bn), lambda i, j: (0, j)),
            pl.BlockSpec((1, bn), lambda i, j: (0, j)),
            tile,
            tile,
        ],
        out_specs=[tile, tile],
        out_shape=[jax.ShapeDtypeStruct((m, f), _BF16), jax.ShapeDtypeStruct((m, f), _F32)],
        compiler_params=_cparams(2),
        name="ffn_up_step",
    )(xn, wg, wu, conv_k8, conv_b, hist0, hist1)


def _pool_mix_groups(window_sum, u, wp_ref, scale, o_ref, cnt_of):
    gw = u.shape[1] // len(_POOL_WINDOWS)
    for g, win in enumerate(_POOL_WINDOWS):
        cols = slice(g * gw, (g + 1) * gw)
        pooled = window_sum(cols, win) / cnt_of(win) - u[:, cols]
        mixed = _dot(pooled.astype(_BF16), wp_ref[g]) * scale[:, cols]
        o_ref[:, cols] = mixed.astype(o_ref.dtype)


def _pool_seq_kernel(u_ref, wp_ref, sc_ref, o_ref, ext_ref, *, hist):
    tb = u_ref.shape[0]
    t0 = pl.program_id(1) * tb
    u = u_ref[...]

    @pl.when(pl.program_id(1) == 0)
    def _():
        ext_ref[0:hist, :] = jnp.zeros((hist, u.shape[1]), _F32)

    ext_ref[hist:, :] = u

    def window_sum(cols, win):
        s = ext_ref[:, cols]
        sh = 1
        while sh < win:
            s = s + pltpu.roll(s, sh, 0)
            sh *= 2
        return s[hist:, :]

    pos = t0 + lax.broadcasted_iota(jnp.int32, (tb, 1), 0)

    def cnt_of(win):
        return jnp.minimum(pos + 1, win).astype(_F32)

    _pool_mix_groups(window_sum, u, wp_ref, sc_ref[...], o_ref, cnt_of)
    ext_ref[0:hist, :] = u[tb - hist:, :]


def _pool_seq(u, w_pool, scale, batch, seq_len):
    m, c = u.shape
    hist = 2 * _SUBLANE
    tb = _block(seq_len, 256)
    nt = seq_len // tb
    g, gw, _ = w_pool.shape
    return pl.pallas_call(
        functools.partial(_pool_seq_kernel, hist=hist),
        grid=(batch, nt),
        in_specs=[
            pl.BlockSpec((tb, c), lambda b, t: (b * nt + t, 0)),
            pl.BlockSpec((g, gw, gw), lambda b, t: (0, 0, 0)),
            pl.BlockSpec((1, c), lambda b, t: (0, 0)),
        ],
        out_specs=pl.BlockSpec((tb, c), lambda b, t: (b * nt + t, 0)),
        out_shape=jax.ShapeDtypeStruct((m, c), _BF16),
        scratch_shapes=[pltpu.VMEM((hist + tb, c), _F32)],
        compiler_params=_cparams(2),
        name="pool_seq",
    )(u, w_pool, scale)


def _pool_step_kernel(ext_ref, wp_ref, sc_ref, o_ref, *, pos):
    ext = ext_ref[...]
    rows = ext.shape[1]
    u = ext[:, rows - 1, :]

    def window_sum(cols, win):
        return jnp.sum(ext[:, rows - win:, cols], axis=1)

    def cnt_of(win):
        return float(min(pos + 1, win))

    _pool_mix_groups(window_sum, u, wp_ref, sc_ref[...], o_ref, cnt_of)


def _pool_step(ext, w_pool, scale, pos):
    m, rows, c = ext.shape
    bb = _block(m, 32)
    g, gw, _ = w_pool.shape
    return pl.pallas_call(
        functools.partial(_pool_step_kernel, pos=pos),
        grid=(m // bb,),
        in_specs=[
            pl.BlockSpec((bb, rows, c), lambda i: (i, 0, 0)),
            pl.BlockSpec((g, gw, gw), lambda i: (0, 0, 0)),
            pl.BlockSpec((1, c), lambda i: (0, 0)),
        ],
        out_specs=pl.BlockSpec((bb, c), lambda i: (i, 0)),
        out_shape=jax.ShapeDtypeStruct((m, c), _BF16),
        compiler_params=_cparams(1),
        name="pool_step",
    )(ext, w_pool, scale)


def _segsum(x, e_ref):
    e = e_ref[...]
    w = e.shape[0]
    hi = x.astype(_BF16)
    lo = (x - hi.astype(_F32)).astype(_BF16)
    parts = []
    for c in range(x.shape[1] // w):
        cols = slice(c * w, (c + 1) * w)
        parts.append(_dot(hi[:, cols], e) + _dot(lo[:, cols], e))
    return jnp.concatenate(parts, axis=1)


def _rwkv_prep(za, vec_ref, wd_ref, wa_ref, wg_ref, e_ref, out_refs):
    da = wd_ref.shape[1]
    lw, la, lg = wd_ref.shape[0], wa_ref.shape[0], wg_ref.shape[0]
    vec = vec_ref[...]
    w0, a0, k_k, k_a = vec[0:1], vec[1:2], vec[2:3], vec[3:4]
    r, k, v = za[:, :da], za[:, da:2 * da], za[:, 2 * da:3 * da]
    o = 3 * da
    xw, xa, xg = za[:, o:o + lw], za[:, o + lw:o + lw + la], za[:, o + lw + la:o + lw + la + lg]
    w = -_softplus(-(w0 + _dot(jnp.tanh(xw).astype(_BF16), wd_ref[...]))) - 0.5
    decay = jnp.exp(-jnp.exp(w))
    a = _sigmoid(a0 + _dot(xa.astype(_BF16), wa_ref[...]))
    g = _dot(_sigmoid(xg).astype(_BF16), wg_ref[...])
    kk = k * k_k
    kk = kk / jnp.maximum(jnp.sqrt(_segsum(kk * kk, e_ref)), _KK_FLOOR)
    k2 = k * (1.0 + (a - 1.0) * k_a)
    for ref, val in zip(out_refs, (r, decay, k2, v, -kk, kk * a, g)):
        ref[...] = val


def _prep_seq_kernel(y_ref, mu_ref, vec_ref, wd_ref, wa_ref, wg_ref, e_ref, *rest):
    out_refs, carry_ref = rest[:7], rest[7]

    @pl.when(pl.program_id(1) == 0)
    def _():
        carry_ref[...] = jnp.zeros(carry_ref.shape, _F32)

    y = y_ref[...]
    row = lax.broadcasted_iota(jnp.int32, y.shape, 0)
    prev = jnp.where(row == 0, carry_ref[_SUBLANE - 1:_SUBLANE, :], pltpu.roll(y, 1, 0))
    carry_ref[...] = y[y.shape[0] - _SUBLANE:, :]
    za = y + mu_ref[...] * (prev - y)
    _rwkv_prep(za, vec_ref, wd_ref, wa_ref, wg_ref, e_ref, out_refs)


def _prep_step_kernel(y_ref, prev_ref, mu_ref, vec_ref, wd_ref, wa_ref, wg_ref, e_ref, *out_refs):
    y = y_ref[...]
    za = y + mu_ref[...] * (prev_ref[...] - y)
    _rwkv_prep(za, vec_ref, wd_ref, wa_ref, wg_ref, e_ref, out_refs)


def _prep_common_specs(ap, da, lw, la, lg, ew, idx):
    return [
        pl.BlockSpec((1, ap), idx),
        pl.BlockSpec((_SUBLANE, da), idx),
        pl.BlockSpec((lw, da), idx),
        pl.BlockSpec((la, da), idx),
        pl.BlockSpec((lg, da), idx),
        pl.BlockSpec((ew, ew), idx),
    ]


def _prep_seq(y, mu, vec, wd, wa, wg, e, batch, seq_len):
    m, ap = y.shape
    da = wd.shape[1]
    tb = _block(seq_len, 128)
    nt = seq_len // tb
    const = lambda b, t: (0, 0)
    rows = lambda b, t: (b * nt + t, 0)
    return pl.pallas_call(
        _prep_seq_kernel,
        grid=(batch, nt),
        in_specs=[pl.BlockSpec((tb, ap), rows)] + _prep_common_specs(ap, da, wd.shape[0], wa.shape[0], wg.shape[0], e.shape[0], const),
        out_specs=[pl.BlockSpec((tb, da), rows)] * 7,
        out_shape=[jax.ShapeDtypeStruct((m, da), _F32)] * 7,
        scratch_shapes=[pltpu.VMEM((_SUBLANE, ap), _F32)],
        compiler_params=_cparams(2),
        name="rwkv_prep_seq",
    )(y, mu, vec, wd, wa, wg, e)


def _prep_step(y, prev, mu, vec, wd, wa, wg, e):
    m, ap = y.shape
    da = wd.shape[1]
    tb = _block(m, 128)
    const = lambda i: (0, 0)
    rows = lambda i: (i, 0)
    return pl.pallas_call(
        _prep_step_kernel,
        grid=(m // tb,),
        in_specs=[pl.BlockSpec((tb, ap), rows)] * 2 + _prep_common_specs(ap, da, wd.shape[0], wa.shape[0], wg.shape[0], e.shape[0], const),
        out_specs=[pl.BlockSpec((tb, da), rows)] * 7,
        out_shape=[jax.ShapeDtypeStruct((m, da), _F32)] * 7,
        compiler_params=_cparams(1),
        name="rwkv_prep_step",
    )(y, prev, mu, vec, wd, wa, wg, e)


def _post_kernel(y_ref, r_ref, k_ref, v_ref, g_ref, vec_ref, e_ref, o_ref, *, head):
    y = y_ref[...]
    vec = vec_ref[...]
    ln_g, ln_b, r_k = vec[0:1], vec[1:2], vec[2:3]
    inv = 1.0 / head
    mu = _segsum(y, e_ref) * inv
    d = y - mu
    var = _segsum(d * d, e_ref) * inv
    yn = d * lax.rsqrt(var + _GN_EPS) * ln_g + ln_b
    v = v_ref[...]
    bonus = _segsum(r_ref[...] * k_ref[...] * r_k, e_ref) * v
    o_ref[...] = ((yn + bonus) * g_ref[...]).astype(o_ref.dtype)


def _post(y, r, k, v, g, vec, e, head):
    m, da = y.shape
    tb = _block(m, 256)
    rows = lambda i: (i, 0)
    const = lambda i: (0, 0)
    return pl.pallas_call(
        functools.partial(_post_kernel, head=head),
        grid=(m // tb,),
        in_specs=[pl.BlockSpec((tb, da), rows)] * 5 + [pl.BlockSpec((_SUBLANE, da), const), pl.BlockSpec(e.shape, const)],
        out_specs=pl.BlockSpec((tb, da), rows),
        out_shape=jax.ShapeDtypeStruct((m, da), _BF16),
        compiler_params=_cparams(1),
        name="rwkv_post",
    )(y, r, k, v, g, vec, e)


_SCAN_ROWS = 4


def _wkv_scan_kernel(r_ref, w_ref, k_ref, v_ref, a_ref, b_ref, y_ref, s_ref):
    tc = r_ref.shape[0]
    nv = s_ref.shape[0]

    @pl.when(pl.program_id(0) == 0)
    def _():
        s_ref[...] = jnp.zeros(s_ref.shape, _F32)

    for vb in range(nv // _SCAN_ROWS):
        rows = slice(vb * _SCAN_ROWS, (vb + 1) * _SCAN_ROWS)

        def step(t, s, rows=rows):
            a, w, b, k, r = a_ref[t], w_ref[t], b_ref[t], k_ref[t], r_ref[t]
            vv = v_ref[t, rows, :]
            sa = jnp.sum(s * a[None], axis=1, keepdims=True)
            s = s * w[None] + sa * b[None] + vv[:, None, :] * k[None]
            y_ref[t, rows, :] = jnp.sum(s * r[None], axis=1)
            return s

        s_ref[rows] = lax.fori_loop(0, tc, step, s_ref[rows], unroll=2)


def _wkv_scan(r, w, k, v, a, b):
    t, n, lanes = r.shape
    tc = _block(t, 64)
    blk = pl.BlockSpec((tc, n, lanes), lambda c: (c, 0, 0))
    return pl.pallas_call(
        _wkv_scan_kernel,
        grid=(t // tc,),
        in_specs=[blk] * 6,
        out_specs=[blk, pl.BlockSpec((n, n, lanes), lambda c: (0, 0, 0))],
        out_shape=[jax.ShapeDtypeStruct((t, n, lanes), _F32), jax.ShapeDtypeStruct((n, n, lanes), _F32)],
        compiler_params=_cparams(1),
        name="wkv_scan",
    )(r, w, k, v, a, b)


def _wkv_step_kernel(s_ref, r_ref, w_ref, k_ref, v_ref, a_ref, b_ref, eye_ref, y_ref, so_ref):
    eye = eye_ref[...]

    def head(i, carry):
        row = pl.ds(i, 1)
        s = s_ref[i]
        sa = jnp.sum(s * a_ref[row, :], axis=1, keepdims=True)
        v_col = jnp.sum(v_ref[row, :] * eye, axis=1, keepdims=True)
        s = s * w_ref[row, :] + sa * b_ref[row, :] + v_col * k_ref[row, :]
        so_ref[i] = s
        y_col = jnp.sum(s * r_ref[row, :], axis=1, keepdims=True)
        y_ref[row, :] = jnp.sum(y_col * eye, axis=0, keepdims=True)
        return carry

    lax.fori_loop(0, s_ref.shape[0], head, 0)


def _wkv_step(s, r, w, k, v, a, b):
    nh, n, _ = s.shape
    hb = _block(nh, 128)
    vec = pl.BlockSpec((hb, n), lambda i: (i, 0))
    st = pl.BlockSpec((hb, n, n), lambda i: (i, 0, 0))
    eye = jnp.eye(n, dtype=_F32)
    return pl.pallas_call(
        _wkv_step_kernel,
        grid=(nh // hb,),
        in_specs=[st] + [vec] * 6 + [pl.BlockSpec((n, n), lambda i: (0, 0))],
        out_specs=[vec, st],
        out_shape=[jax.ShapeDtypeStruct((nh, n), _F32), jax.ShapeDtypeStruct((nh, n, n), _F32)],
        compiler_params=_cparams(1),
        name="wkv_step",
    )(s, r, w, k, v, a, b, eye)


def _pad_rows(w, rows):
    return jnp.pad(w, ((0, rows - w.shape[0]), (0, 0)))


def _pad_cols(w, cols):
    return jnp.pad(w, ((0, 0), (0, cols - w.shape[1])))


def _up(n, mult):
    return -(-n // mult) * mult


def _layer_weights(norm_mix, w_in, mu_shift, w0, w_decay_up, a0, w_aaa_up, w_gate_up, k_k, k_a, r_k,
                   ln_x_g, ln_x_b, w_o_a, w_pool, pool_scale, w_o_b, w_out, norm_ffn, w_ffn_gate, w_ffn_up,
                   conv_k, conv_b, w_ffn_down, norm_ple, w_ple_gate, w_ple):
    da = w0.shape[0]
    nh, hs = r_k.shape
    lw, la, lg = w_decay_up.shape[0], w_aaa_up.shape[0], w_gate_up.shape[0]
    lwp, lap, lgp = _up(lw, _LANE), _up(la, _LANE), _up(lg, _LANE)
    a_proj = 3 * da + lw + la + lg
    db = pool_scale.shape[0]
    d = w_in.shape[0]
    o = 3 * da

    def regroup(x):
        return jnp.concatenate([
            x[:, :o], _pad_cols(x[:, o:o + lw], lwp), _pad_cols(x[:, o + lw:o + lw + la], lap),
            _pad_cols(x[:, o + lw + la:a_proj], lgp)], axis=1)

    ew = _MXU_DIM if (da % _MXU_DIM == 0 and _MXU_DIM % hs == 0) else hs
    seg = jnp.arange(ew) // hs
    return dict(
        dims=dict(da=da, nh=nh, hs=hs, lw=lw, la=la, lg=lg, lwp=lwp, lap=lap, a_proj=a_proj, db=db, d=d),
        norm_mix=norm_mix, norm_ffn=norm_ffn, norm_ple=norm_ple,
        w_a=regroup(w_in[:, :a_proj]).astype(_BF16),
        w_u=w_in[:, a_proj:a_proj + db].astype(_BF16),
        w_gates=w_in[:, a_proj + db:].astype(_BF16),
        mu=regroup(mu_shift[None, :]),
        vec_prep=_pad_rows(jnp.stack([w0, a0, k_k, k_a]), _SUBLANE),
        vec_post=_pad_rows(jnp.stack([ln_x_g, ln_x_b, r_k.reshape(-1)]), _SUBLANE),
        wd=_pad_rows(w_decay_up, lwp).astype(_BF16),
        wa=_pad_rows(w_aaa_up, lap).astype(_BF16),
        wg=_pad_rows(w_gate_up, lgp).astype(_BF16),
        seg_ones=(seg[:, None] == seg[None, :]).astype(_BF16),
        w_o_a=w_o_a.astype(_BF16), w_o_b=w_o_b.astype(_BF16), w_out=w_out.astype(_BF16),
        w_pool=w_pool.astype(_BF16), pool_scale=pool_scale[None, :],
        w_ffn_gate=w_ffn_gate.astype(_BF16), w_ffn_up=w_ffn_up.astype(_BF16), w_ffn_down=w_ffn_down.astype(_BF16),
        conv_k=_pad_rows(conv_k, _SUBLANE), conv_b=conv_b[None, :],
        w_ple_gate=w_ple_gate.astype(_BF16), w_ple=w_ple.astype(_BF16),
    )


def _ungroup(za, dm):
    o = 3 * dm["da"]
    o2 = o + dm["lwp"]
    o3 = o2 + dm["lap"]
    return jnp.concatenate([za[..., :o], za[..., o:o + dm["lw"]], za[..., o2:o2 + dm["la"]],
                            za[..., o3:o3 + dm["lg"]]], axis=-1)


def _mixer_tail(h, p, o_a, o_b, gates, lw, ffn_up):
    merged = _merge(o_a, o_b, lw["w_o_a"], lw["w_o_b"], gates)
    h = _mm_res(merged, lw["w_out"], h, "out_proj")
    hid, conv_state = ffn_up(_rmsnorm(h, lw["norm_ffn"], _BF16))
    h = _mm_res(hid, lw["w_ffn_down"], h, "ffn_down")
    h = _ple(_rmsnorm(h, lw["norm_ple"], _BF16), lw["w_ple_gate"], p.astype(_BF16), lw["w_ple"], h)
    return h, conv_state


def _layer_seq(x, p, lw):
    bsz, t, d = x.shape
    dm = lw["dims"]
    da, nh, hs = dm["da"], dm["nh"], dm["hs"]
    m = bsz * t
    h = x.reshape(m, d)
    xn = _rmsnorm(h, lw["norm_mix"], _BF16)
    za_raw = _mm(xn, lw["w_a"], bn_target=512, name="in_proj_a")
    u = _mm(xn, lw["w_u"], name="in_proj_u")
    gates = _mm(xn, lw["w_gates"], act="sigmoid", name="in_proj_gates")

    r, decay, k, v, na, nb, g = _prep_seq(za_raw, lw["mu"], lw["vec_prep"], lw["wd"], lw["wa"], lw["wg"],
                                          lw["seg_ones"], bsz, t)

    def to_scan(z):
        return z.reshape(bsz, t, nh, hs).transpose(1, 3, 0, 2).reshape(t, hs, bsz * nh)

    y, s_fin = _wkv_scan(*(to_scan(z) for z in (r, decay, k, v, na, nb)))
    y = y.reshape(t, hs, bsz, nh).transpose(2, 0, 3, 1).reshape(m, da)
    new_wkv = s_fin.reshape(hs, hs, bsz, nh).transpose(2, 3, 0, 1)
    o_a = _post(y, r, k, v, g, lw["vec_post"], lw["seg_ones"], hs)
    o_b = _pool_seq(u, lw["w_pool"], lw["pool_scale"], bsz, t)

    def ffn_up(xn2):
        hid, tails = _ffn_up_seq(xn2, lw["w_ffn_gate"], lw["w_ffn_up"], lw["conv_k"], lw["conv_b"], t)
        per_seq = tails.shape[0] // bsz
        return hid, tails[per_seq - 1::per_seq]

    h, tails = _mixer_tail(h, p.reshape(m, -1), o_a, o_b, gates, lw, ffn_up)
    new_shift = _ungroup(za_raw.reshape(bsz, t, -1)[:, t - 1:, :], dm)
    n_hist = max(_POOL_WINDOWS) - 1
    new_pool = u.reshape(bsz, t, -1)[:, t - n_hist:, :]
    return h.reshape(bsz, t, d), new_wkv, new_shift, new_pool, tails


def _layer_step(x, p, s_wkv, s_shift, s_pool, s_conv, pos, lw):
    bsz, _, d = x.shape
    dm = lw["dims"]
    da, nh, hs = dm["da"], dm["nh"], dm["hs"]
    h = x.reshape(bsz, d)
    xn = _rmsnorm(h, lw["norm_mix"], _BF16)
    za_raw = _mm(xn, lw["w_a"], bn_target=512, name="in_proj_a")
    u = _mm(xn, lw["w_u"], name="in_proj_u")
    gates = _mm(xn, lw["w_gates"], act="sigmoid", name="in_proj_gates")

    o = 3 * da
    sh = s_shift.reshape(bsz, -1)
    prev = jnp.concatenate([sh[:, :o], _pad_cols(sh[:, o:o + dm["lw"]], dm["lwp"]),
                            _pad_cols(sh[:, o + dm["lw"]:o + dm["lw"] + dm["la"]], dm["lap"]),
                            _pad_cols(sh[:, o + dm["lw"] + dm["la"]:], za_raw.shape[1] - o - dm["lwp"] - dm["lap"])], axis=1)
    r, decay, k, v, na, nb, g = _prep_step(za_raw, prev, lw["mu"], lw["vec_prep"], lw["wd"], lw["wa"], lw["wg"],
                                           lw["seg_ones"])
    heads = lambda z: z.reshape(bsz * nh, hs)
    y, s_new = _wkv_step(s_wkv.reshape(bsz * nh, hs, hs), *(heads(z) for z in (r, decay, k, v, na, nb)))
    o_a = _post(y.reshape(bsz, da), r, k, v, g, lw["vec_post"], lw["seg_ones"], hs)

    ext = jnp.concatenate([s_pool, u[:, None, :]], axis=1)
    o_b = _pool_step(ext, lw["w_pool"], lw["pool_scale"], pos)

    def ffn_up(xn2):
        return _ffn_up_step(xn2, lw["w_ffn_gate"], lw["w_ffn_up"], lw["conv_k"], lw["conv_b"],
                            s_conv[:, 0, :], s_conv[:, 1, :])

    h, hg = _mixer_tail(h, p.reshape(bsz, -1), o_a, o_b, gates, lw, ffn_up)
    new_shift = _ungroup(za_raw, dm)[:, None, :]
    new_conv = jnp.concatenate([s_conv[:, 1:, :], hg[:, None, :]], axis=1)
    return h.reshape(bsz, 1, d), s_new.reshape(bsz, nh, hs, hs), new_shift, ext[:, 1:, :], new_conv


def kernel(x_prompt, x_sample, p_prompt, p_sample, state_wkv, state_shift, state_pool, state_conv, norm_mix, w_in, mu_shift, w0, w_decay_up, a0, w_aaa_up, w_gate_up, k_k, k_a, r_k, ln_x_g, ln_x_b, w_o_a, w_pool, pool_scale, w_o_b, w_out, norm_ffn, w_ffn_gate, w_ffn_up, conv_k, conv_b, w_ffn_down, norm_ple, w_ple_gate, w_ple, norm_final):
    depth = w_in.shape[0]
    conv_w = conv_k.shape[1]
    per_layer = (norm_mix, w_in, mu_shift, w0, w_decay_up, a0, w_aaa_up, w_gate_up, k_k, k_a, r_k, ln_x_g, ln_x_b,
                 w_o_a, w_pool, pool_scale, w_o_b, w_out, norm_ffn, w_ffn_gate, w_ffn_up, conv_k, conv_b,
                 w_ffn_down, norm_ple, w_ple_gate, w_ple)
    hp, hs = x_prompt, x_sample
    outs_p, outs_s = ([], [], [], []), ([], [], [], [])
    for i in range(depth):
        lw = _layer_weights(*(arr[i] for arr in per_layer))
        hp, wkv_p, shift_p, pool_p, tails_p = _layer_seq(hp, p_prompt[i], lw)
        conv_p = tails_p[:, _SUBLANE - (conv_w - 1):, :]
        hs, wkv_s, shift_s, pool_s, conv_s = _layer_step(hs, p_sample[i], state_wkv[i], state_shift[i],
                                                         state_pool[i], state_conv[i], _PAST_LEN, lw)
        for lst, s in zip(outs_p, (wkv_p, shift_p, pool_p, conv_p)):
            lst.append(s)
        for lst, s in zip(outs_s, (wkv_s, shift_s, pool_s, conv_s)):
            lst.append(s)
    y_prompt = _rmsnorm(hp.reshape(-1, hp.shape[-1]), norm_final, hp.dtype).reshape(hp.shape)
    y_sample = _rmsnorm(hs.reshape(-1, hs.shape[-1]), norm_final, hs.dtype).reshape(hs.shape)
    return (y_prompt, y_sample,
            jnp.stack(outs_p[0]), jnp.stack(outs_p[1]), jnp.stack(outs_p[2]), jnp.stack(outs_p[3]),
            jnp.stack(outs_s[0]), jnp.stack(outs_s[1]), jnp.stack(outs_s[2]), jnp.stack(outs_s[3]))
```

```python
import functools

import jax
import jax.numpy as jnp
from jax import lax
from jax.experimental import pallas as pl
from jax.experimental.pallas import tpu as pltpu

_EPS = 1e-6
_GN_EPS = 64e-5
_POOL_WINDOWS = (2, 4, 8, 16)
_PAST_LEN = 16384
_KK_FLOOR = 1e-12

_LANE = 128
_SUBLANE = 8
_MXU_DIM = 256
_VMEM_LIMIT_BYTES = 56 * 1024 * 1024

_F32 = jnp.float32
_BF16 = jnp.bfloat16


def _cparams(n_grid):
    return pltpu.CompilerParams(dimension_semantics=("arbitrary",) * n_grid,
                                vmem_limit_bytes=_VMEM_LIMIT_BYTES)


def _block(n, target):
    if n <= target:
        return n
    b = target
    while b >= _SUBLANE:
        if n % b == 0 and b % _SUBLANE == 0:
            return b
        b -= _SUBLANE
    return n


def _sigmoid(x):
    return 1.0 / (1.0 + jnp.exp(-x))


def _softplus(x):
    return jnp.maximum(x, 0.0) + jnp.log1p(jnp.exp(-jnp.abs(x)))


def _gelu_tanh(x):
    return x * (0.5 * (1.0 + jnp.tanh(0.7978845608028654 * (x + 0.044715 * (x * x * x)))))


def _dot(a, b):
    return jnp.dot(a, b, preferred_element_type=_F32)


def _rmsnorm_kernel(x_ref, g_ref, o_ref):
    x = x_ref[...]
    y = x * lax.rsqrt(jnp.mean(x * x, axis=-1, keepdims=True) + _EPS)
    o_ref[...] = (y * g_ref[...]).astype(o_ref.dtype)


def _rmsnorm(x, g, out_dtype):
    m, d = x.shape
    bm = _block(m, 256)
    return pl.pallas_call(
        _rmsnorm_kernel,
        grid=(m // bm,),
        in_specs=[pl.BlockSpec((bm, d), lambda i: (i, 0)), pl.BlockSpec((1, d), lambda i: (0, 0))],
        out_specs=pl.BlockSpec((bm, d), lambda i: (i, 0)),
        out_shape=jax.ShapeDtypeStruct((m, d), out_dtype),
        compiler_params=_cparams(1),
        name="rmsnorm",
    )(x, g.reshape(1, d))


def _mm_kernel(x_ref, w_ref, o_ref, *, act):
    acc = _dot(x_ref[...], w_ref[...])
    if act == "sigmoid":
        acc = _sigmoid(acc)
    o_ref[...] = acc.astype(o_ref.dtype)


def _mm(x, w, *, act=None, out_dtype=_F32, bn_target=1024, name="mm"):
    m, k = x.shape
    n = w.shape[1]
    bm, bn = _block(m, 1024), _block(n, bn_target)
    return pl.pallas_call(
        functools.partial(_mm_kernel, act=act),
        grid=(m // bm, n // bn),
        in_specs=[pl.BlockSpec((bm, k), lambda i, j: (i, 0)), pl.BlockSpec((k, bn), lambda i, j: (0, j))],
        out_specs=pl.BlockSpec((bm, bn), lambda i, j: (i, j)),
        out_shape=jax.ShapeDtypeStruct((m, n), out_dtype),
        compiler_params=_cparams(2),
        name=name,
    )(x, w)


def _merge_kernel(oa_ref, ob_ref, wa_ref, wb_ref, ga_ref, gb_ref, o_ref):
    a = _dot(oa_ref[...], wa_ref[...])
    b = _dot(ob_ref[...], wb_ref[...])
    o_ref[...] = (ga_ref[...] * a + gb_ref[...] * b).astype(o_ref.dtype)


def _merge(oa, ob, wa, wb, gates):
    m, ka = oa.shape
    kb = ob.shape[1]
    n = wa.shape[1]
    bm, bn = _block(m, 1024), _block(n, 512)
    nj = n // bn
    return pl.pallas_call(
        _merge_kernel,
        grid=(m // bm, nj),
        in_specs=[
            pl.BlockSpec((bm, ka), lambda i, j: (i, 0)),
            pl.BlockSpec((bm, kb), lambda i, j: (i, 0)),
            pl.BlockSpec((ka, bn), lambda i, j: (0, j)),
            pl.BlockSpec((kb, bn), lambda i, j: (0, j)),
            pl.BlockSpec((bm, bn), lambda i, j: (i, j)),
            pl.BlockSpec((bm, bn), lambda i, j: (i, j + nj)),
        ],
        out_specs=pl.BlockSpec((bm, bn), lambda i, j: (i, j)),
        out_shape=jax.ShapeDtypeStruct((m, n), _BF16),
        compiler_params=_cparams(2),
        name="merge",
    )(oa, ob, wa, wb, gates, gates)


def _mm_res_kernel(x_ref, w_ref, r_ref, o_ref):
    o_ref[...] = r_ref[...] + _dot(x_ref[...], w_ref[...])


def _mm_res_ktiled_kernel(x_ref, w_ref, r_ref, o_ref, acc_ref):
    kk = pl.program_id(2)

    @pl.when(kk == 0)
    def _():
        acc_ref[...] = r_ref[...]

    acc_ref[...] += _dot(x_ref[...], w_ref[...])

    @pl.when(kk == pl.num_programs(2) - 1)
    def _():
        o_ref[...] = acc_ref[...]


def _mm_res(x, w, res, name):
    m, k = x.shape
    n = w.shape[1]
    bm, bn, bk = _block(m, 1024), _block(n, 512), _block(k, 4096)
    if bk == k:
        return pl.pallas_call(
            _mm_res_kernel,
            grid=(m // bm, n // bn),
            in_specs=[
                pl.BlockSpec((bm, k), lambda i, j: (i, 0)),
                pl.BlockSpec((k, bn), lambda i, j: (0, j)),
                pl.BlockSpec((bm, bn), lambda i, j: (i, j)),
            ],
            out_specs=pl.BlockSpec((bm, bn), lambda i, j: (i, j)),
            out_shape=jax.ShapeDtypeStruct((m, n), _F32),
            compiler_params=_cparams(2),
            name=name,
        )(x, w, res)
    return pl.pallas_call(
        _mm_res_ktiled_kernel,
        grid=(m // bm, n // bn, k // bk),
        in_specs=[
            pl.BlockSpec((bm, bk), lambda i, j, q: (i, q)),
            pl.BlockSpec((bk, bn), lambda i, j, q: (q, j)),
            pl.BlockSpec((bm, bn), lambda i, j, q: (i, j)),
        ],
        out_specs=pl.BlockSpec((bm, bn), lambda i, j, q: (i, j)),
        out_shape=jax.ShapeDtypeStruct((m, n), _F32),
        scratch_shapes=[pltpu.VMEM((bm, bn), _F32)],
        compiler_params=_cparams(3),
        name=name,
    )(x, w, res)


def _ple_kernel(x_ref, wg_ref, p_ref, wp_ref, r_ref, o_ref):
    gate = _sigmoid(_dot(x_ref[...], wg_ref[...]))
    o_ref[...] = r_ref[...] + gate * _dot(p_ref[...], wp_ref[...])


def _ple(xn, wg, p, wp, res):
    m, k = xn.shape
    kp = p.shape[1]
    n = wg.shape[1]
    bm, bn = _block(m, 1024), _block(n, 512)
    return pl.pallas_call(
        _ple_kernel,
        grid=(m // bm, n // bn),
        in_specs=[
            pl.BlockSpec((bm, k), lambda i, j: (i, 0)),
            pl.BlockSpec((k, bn), lambda i, j: (0, j)),
            pl.BlockSpec((bm, kp), lambda i, j: (i, 0)),
            pl.BlockSpec((kp, bn), lambda i, j: (0, j)),
            pl.BlockSpec((bm, bn), lambda i, j: (i, j)),
        ],
        out_specs=pl.BlockSpec((bm, bn), lambda i, j: (i, j)),
        out_shape=jax.ShapeDtypeStruct((m, n), _F32),
        compiler_params=_cparams(2),
        name="ple",
    )(xn, wg, p, wp, res)


def _ffn_up_seq_kernel(x_ref, wg_ref, wu_ref, ck_ref, cb_ref, h_ref, tail_ref, carry_ref, *, blocks_per_seq):
    i, j = pl.program_id(0), pl.program_id(1)

    @pl.when(i % blocks_per_seq == 0)
    def _():
        carry_ref[j] = jnp.zeros(carry_ref.shape[1:], _F32)

    hist = carry_ref[j]
    x = x_ref[...]
    hg = _dot(x, wg_ref[...])
    up = _dot(x, wu_ref[...])
    bm = hg.shape[0]
    row = lax.broadcasted_iota(jnp.int32, hg.shape, 0)
    prev1 = jnp.where(row == 0, hist[7:8], pltpu.roll(hg, 1, 0))
    prev2 = jnp.where(row == 0, hist[6:7], jnp.where(row == 1, hist[7:8], pltpu.roll(hg, 2, 0)))
    ck = ck_ref[...]
    conv = cb_ref[...] + (prev2 * ck[0:1] + prev1 * ck[1:2] + hg * ck[2:3])
    h_ref[...] = (_gelu_tanh(conv) * up).astype(h_ref.dtype)
    tail = hg[bm - _SUBLANE:, :]
    tail_ref[0] = tail
    carry_ref[j] = tail


def _ffn_up_seq(xn, wg, wu, conv_k8, conv_b, seq_len):
    m, k = xn.shape
    f = wg.shape[1]
    bm, bn = _block(seq_len, 1024), _block(f, 512)
    nj = f // bn
    return pl.pallas_call(
        functools.partial(_ffn_up_seq_kernel, blocks_per_seq=seq_len // bm),
        grid=(m // bm, nj),
        in_specs=[
            pl.BlockSpec((bm, k), lambda i, j: (i, 0)),
            pl.BlockSpec((k, bn), lambda i, j: (0, j)),
            pl.BlockSpec((k, bn), lambda i, j: (0, j)),
            pl.BlockSpec((_SUBLANE, bn), lambda i, j: (0, j)),
            pl.BlockSpec((1, bn), lambda i, j: (0, j)),
        ],
        out_specs=[
            pl.BlockSpec((bm, bn), lambda i, j: (i, j)),
            pl.BlockSpec((1, _SUBLANE, bn), lambda i, j: (i, 0, j)),
        ],
        out_shape=[
            jax.ShapeDtypeStruct((m, f), _BF16),
            jax.ShapeDtypeStruct((m // bm, _SUBLANE, f), _F32),
        ],
        scratch_shapes=[pltpu.VMEM((nj, _SUBLANE, bn), _F32)],
        compiler_params=_cparams(2),
        name="ffn_up_seq",
    )(xn, wg, wu, conv_k8, conv_b)


def _ffn_up_step_kernel(x_ref, wg_ref, wu_ref, ck_ref, cb_ref, h0_ref, h1_ref, h_ref, hg_ref):
    x = x_ref[...]
    hg = _dot(x, wg_ref[...])
    up = _dot(x, wu_ref[...])
    ck = ck_ref[...]
    c = cb_ref[...] + (h0_ref[...] * ck[0:1] + h1_ref[...] * ck[1:2] + hg * ck[2:3])
    h_ref[...] = (_gelu_tanh(c) * up).astype(h_ref.dtype)
    hg_ref[...] = hg


def _ffn_up_step(xn, wg, wu, conv_k8, conv_b, hist0, hist1):
    m, k = xn.shape
    f = wg.shape[1]
    bm, bn = _block(m, 1024), _block(f, 512)
    tile = pl.BlockSpec((bm, bn), lambda i, j: (i, j))
    return pl.pallas_call(
        _ffn_up_step_kernel,
        grid=(m // bm, f // bn),
        in_specs=[
            pl.BlockSpec((bm, k), lambda i, j: (i, 0)),
            pl.BlockSpec((k, bn), lambda i, j: (0, j)),
            pl.BlockSpec((k, bn), lambda i, j: (0, j)),
            pl.BlockSpec((_SUBLANE, bn), lambda i, j: (0, j)),
            pl.BlockSpec((1, bn), lambda i, j: (0, j)),
            tile,
            tile,
        ],
        out_specs=[tile, tile],
        out_shape=[jax.ShapeDtypeStruct((m, f), _BF16), jax.ShapeDtypeStruct((m, f), _F32)],
        compiler_params=_cparams(2),
        name="ffn_up_step",
    )(xn, wg, wu, conv_k8, conv_b, hist0, hist1)


def _pool_mix_groups(window_sum, u, wp_ref, scale, o_ref, cnt_of):
    gw = u.shape[1] // len(_POOL_WINDOWS)
    for g, win in enumerate(_POOL_WINDOWS):
        cols = slice(g * gw, (g + 1) * gw)
        pooled = window_sum(cols, win) / cnt_of(win) - u[:, cols]
        mixed = _dot(pooled.astype(_BF16), wp_ref[g]) * scale[:, cols]
        o_ref[:, cols] = mixed.astype(o_ref.dtype)


def _pool_seq_kernel(u_ref, wp_ref, sc_ref, o_ref, ext_ref, *, hist):
    tb = u_ref.shape[0]
    t0 = pl.program_id(1) * tb
    u = u_ref[...]

    @pl.when(pl.program_id(1) == 0)
    def _():
        ext_ref[0:hist, :] = jnp.zeros((hist, u.shape[1]), _F32)

    ext_ref[hist:, :] = u

    def window_sum(cols, win):
        s = ext_ref[:, cols]
        sh = 1
        while sh < win:
            s = s + pltpu.roll(s, sh, 0)
            sh *= 2
        return s[hist:, :]

    pos = t0 + lax.broadcasted_iota(jnp.int32, (tb, 1), 0)

    def cnt_of(win):
        return jnp.minimum(pos + 1, win).astype(_F32)

    _pool_mix_groups(window_sum, u, wp_ref, sc_ref[...], o_ref, cnt_of)
    ext_ref[0:hist, :] = u[tb - hist:, :]


def _pool_seq(u, w_pool, scale, batch, seq_len):
    m, c = u.shape
    hist = 2 * _SUBLANE
    tb = _block(seq_len, 256)
    nt = seq_len // tb
    g, gw, _ = w_pool.shape
    return pl.pallas_call(
        functools.partial(_pool_seq_kernel, hist=hist),
        grid=(batch, nt),
        in_specs=[
            pl.BlockSpec((tb, c), lambda b, t: (b * nt + t, 0)),
            pl.BlockSpec((g, gw, gw), lambda b, t: (0, 0, 0)),
            pl.BlockSpec((1, c), lambda b, t: (0, 0)),
        ],
        out_specs=pl.BlockSpec((tb, c), lambda b, t: (b * nt + t, 0)),
        out_shape=jax.ShapeDtypeStruct((m, c), _BF16),
        scratch_shapes=[pltpu.VMEM((hist + tb, c), _F32)],
        compiler_params=_cparams(2),
        name="pool_seq",
    )(u, w_pool, scale)


def _pool_step_kernel(ext_ref, wp_ref, sc_ref, o_ref, *, pos):
    ext = ext_ref[...]
    rows = ext.shape[1]
    u = ext[:, rows - 1, :]

    def window_sum(cols, win):
        return jnp.sum(ext[:, rows - win:, cols], axis=1)

    def cnt_of(win):
        return float(min(pos + 1, win))

    _pool_mix_groups(window_sum, u, wp_ref, sc_ref[...], o_ref, cnt_of)


def _pool_step(ext, w_pool, scale, pos):
    m, rows, c = ext.shape
    bb = _block(m, 32)
    g, gw, _ = w_pool.shape
    return pl.pallas_call(
        functools.partial(_pool_step_kernel, pos=pos),
        grid=(m // bb,),
        in_specs=[
            pl.BlockSpec((bb, rows, c), lambda i: (i, 0, 0)),
            pl.BlockSpec((g, gw, gw), lambda i: (0, 0, 0)),
            pl.BlockSpec((1, c), lambda i: (0, 0)),
        ],
        out_specs=pl.BlockSpec((bb, c), lambda i: (i, 0)),
        out_shape=jax.ShapeDtypeStruct((m, c), _BF16),
        compiler_params=_cparams(1),
        name="pool_step",
    )(ext, w_pool, scale)


def _segsum(x, e_ref):
    e = e_ref[...]
    w = e.shape[0]
    hi = x.astype(_BF16)
    lo = (x - hi.astype(_F32)).astype(_BF16)
    parts = []
    for c in range(x.shape[1] // w):
        cols = slice(c * w, (c + 1) * w)
        parts.append(_dot(hi[:, cols], e) + _dot(lo[:, cols], e))
    return jnp.concatenate(parts, axis=1)


def _rwkv_prep(za, vec_ref, wd_ref, wa_ref, wg_ref, e_ref, out_refs):
    da = wd_ref.shape[1]
    lw, la, lg = wd_ref.shape[0], wa_ref.shape[0], wg_ref.shape[0]
    vec = vec_ref[...]
    w0, a0, k_k, k_a = vec[0:1], vec[1:2], vec[2:3], vec[3:4]
    r, k, v = za[:, :da], za[:, da:2 * da], za[:, 2 * da:3 * da]
    o = 3 * da
    xw, xa, xg = za[:, o:o + lw], za[:, o + lw:o + lw + la], za[:, o + lw + la:o + lw + la + lg]
    w = -_softplus(-(w0 + _dot(jnp.tanh(xw).astype(_BF16), wd_ref[...]))) - 0.5
    decay = jnp.exp(-jnp.exp(w))
    a = _sigmoid(a0 + _dot(xa.astype(_BF16), wa_ref[...]))
    g = _dot(_sigmoid(xg).astype(_BF16), wg_ref[...])
    kk = k * k_k
    kk = kk / jnp.maximum(jnp.sqrt(_segsum(kk * kk, e_ref)), _KK_FLOOR)
    k2 = k * (1.0 + (a - 1.0) * k_a)
    for ref, val in zip(out_refs, (r, decay, k2, v, -kk, kk * a, g)):
        ref[...] = val


def _prep_seq_kernel(y_ref, mu_ref, vec_ref, wd_ref, wa_ref, wg_ref, e_ref, *rest):
    out_refs, carry_ref = rest[:7], rest[7]

    @pl.when(pl.program_id(1) == 0)
    def _():
        carry_ref[...] = jnp.zeros(carry_ref.shape, _F32)

    y = y_ref[...]
    row = lax.broadcasted_iota(jnp.int32, y.shape, 0)
    prev = jnp.where(row == 0, carry_ref[_SUBLANE - 1:_SUBLANE, :], pltpu.roll(y, 1, 0))
    carry_ref[...] = y[y.shape[0] - _SUBLANE:, :]
    za = y + mu_ref[...] * (prev - y)
    _rwkv_prep(za, vec_ref, wd_ref, wa_ref, wg_ref, e_ref, out_refs)


def _prep_step_kernel(y_ref, prev_ref, mu_ref, vec_ref, wd_ref, wa_ref, wg_ref, e_ref, *out_refs):
    y = y_ref[...]
    za = y + mu_ref[...] * (prev_ref[...] - y)
    _rwkv_prep(za, vec_ref, wd_ref, wa_ref, wg_ref, e_ref, out_refs)


def _prep_common_specs(ap, da, lw, la, lg, ew, idx):
    return [
        pl.BlockSpec((1, ap), idx),
        pl.BlockSpec((_SUBLANE, da), idx),
        pl.BlockSpec((lw, da), idx),
        pl.BlockSpec((la, da), idx),
        pl.BlockSpec((lg, da), idx),
        pl.BlockSpec((ew, ew), idx),
    ]


def _prep_seq(y, mu, vec, wd, wa, wg, e, batch, seq_len):
    m, ap = y.shape
    da = wd.shape[1]
    tb = _block(seq_len, 128)
    nt = seq_len // tb
    const = lambda b, t: (0, 0)
    rows = lambda b, t: (b * nt + t, 0)
    return pl.pallas_call(
        _prep_seq_kernel,
        grid=(batch, nt),
        in_specs=[pl.BlockSpec((tb, ap), rows)] + _prep_common_specs(ap, da, wd.shape[0], wa.shape[0], wg.shape[0], e.shape[0], const),
        out_specs=[pl.BlockSpec((tb, da), rows)] * 7,
        out_shape=[jax.ShapeDtypeStruct((m, da), _F32)] * 7,
        scratch_shapes=[pltpu.VMEM((_SUBLANE, ap), _F32)],
        compiler_params=_cparams(2),
        name="rwkv_prep_seq",
    )(y, mu, vec, wd, wa, wg, e)


def _prep_step(y, prev, mu, vec, wd, wa, wg, e):
    m, ap = y.shape
    da = wd.shape[1]
    tb = _block(m, 128)
    const = lambda i: (0, 0)
    rows = lambda i: (i, 0)
    return pl.pallas_call(
        _prep_step_kernel,
        grid=(m // tb,),
        in_specs=[pl.BlockSpec((tb, ap), rows)] * 2 + _prep_common_specs(ap, da, wd.shape[0], wa.shape[0], wg.shape[0], e.shape[0], const),
        out_specs=[pl.BlockSpec((tb, da), rows)] * 7,
        out_shape=[jax.ShapeDtypeStruct((m, da), _F32)] * 7,
        compiler_params=_cparams(1),
        name="rwkv_prep_step",
    )(y, prev, mu, vec, wd, wa, wg, e)


def _post_kernel(y_ref, r_ref, k_ref, v_ref, g_ref, vec_ref, e_ref, o_ref, *, head):
    y = y_ref[...]
    vec = vec_ref[...]
    ln_g, ln_b, r_k = vec[0:1], vec[1:2], vec[2:3]
    inv = 1.0 / head
    mu = _segsum(y, e_ref) * inv
    d = y - mu
    var = _segsum(d * d, e_ref) * inv
    yn = d * lax.rsqrt(var + _GN_EPS) * ln_g + ln_b
    v = v_ref[...]
    bonus = _segsum(r_ref[...] * k_ref[...] * r_k, e_ref) * v
    o_ref[...] = ((yn + bonus) * g_ref[...]).astype(o_ref.dtype)


def _post(y, r, k, v, g, vec, e, head):
    m, da = y.shape
    tb = _block(m, 256)
    rows = lambda i: (i, 0)
    const = lambda i: (0, 0)
    return pl.pallas_call(
        functools.partial(_post_kernel, head=head),
        grid=(m // tb,),
        in_specs=[pl.BlockSpec((tb, da), rows)] * 5 + [pl.BlockSpec((_SUBLANE, da), const), pl.BlockSpec(e.shape, const)],
        out_specs=pl.BlockSpec((tb, da), rows),
        out_shape=jax.ShapeDtypeStruct((m, da), _BF16),
        compiler_params=_cparams(1),
        name="rwkv_post",
    )(y, r, k, v, g, vec, e)


_SCAN_ROWS = 2 * _SUBLANE
_SCAN_PAD = _SUBLANE


def _to_scan_kernel(x_ref, o_ref, s_ref, *, n, pitch):
    nb, tb, c = x_ref.shape
    nh, per_tile = c // n, _LANE // n
    lanes = nb * nh
    for b in range(nb):
        for g in range(c // _LANE):
            tile = x_ref[b, :, g * _LANE:(g + 1) * _LANE].T
            for j in range(per_tile):
                bh = b * nh + g * per_tile + j
                s_ref[bh * pitch:bh * pitch + n, :] = tile[j * n:(j + 1) * n, :]
    for k in range(n):
        o_ref[pl.ds(k, tb, stride=pitch), :] = s_ref[pl.ds(k, lanes, stride=pitch), :].T
    for t in range(tb):
        o_ref[t * pitch + n:(t + 1) * pitch, :] = jnp.zeros((pitch - n, lanes), _F32)


def _to_scan(x, n):
    nb, t, c = x.shape
    lanes = nb * (c // n)
    pitch = n + _SCAN_PAD
    tb = _block(t, _LANE)
    out = pl.pallas_call(
        functools.partial(_to_scan_kernel, n=n, pitch=pitch),
        grid=(t // tb,),
        in_specs=[pl.BlockSpec((nb, tb, c), lambda i: (0, i, 0))],
        out_specs=pl.BlockSpec((tb * pitch, lanes), lambda i: (i, 0)),
        out_shape=jax.ShapeDtypeStruct((t * pitch, lanes), _F32),
        scratch_shapes=[pltpu.VMEM((lanes * pitch, tb), _F32)],
        compiler_params=_cparams(1),
        name="to_scan",
    )(x)
    return out.reshape(t, pitch, lanes)


def _from_scan_kernel(y_ref, o_ref, s_ref, *, n, pitch):
    nb, tb, c = o_ref.shape
    nh, per_tile = c // n, _LANE // n
    lanes = nb * nh
    for v in range(n):
        s_ref[pl.ds(v, lanes, stride=pitch), :] = y_ref[pl.ds(v, tb, stride=pitch), :].T
    for b in range(nb):
        for g in range(c // _LANE):
            bh = b * nh + g * per_tile
            tile = jnp.concatenate([s_ref[(bh + j) * pitch:(bh + j) * pitch + n, :] for j in range(per_tile)], axis=0)
            o_ref[b, :, g * _LANE:(g + 1) * _LANE] = tile.T


def _from_scan(y, nb, n):
    t, pitch, lanes = y.shape
    c = (lanes // nb) * n
    tb = _block(t, _LANE)
    return pl.pallas_call(
        functools.partial(_from_scan_kernel, n=n, pitch=pitch),
        grid=(t // tb,),
        in_specs=[pl.BlockSpec((tb * pitch, lanes), lambda i: (i, 0))],
        out_specs=pl.BlockSpec((nb, tb, c), lambda i: (0, i, 0)),
        out_shape=jax.ShapeDtypeStruct((nb, t, c), _F32),
        scratch_shapes=[pltpu.VMEM((lanes * pitch, tb), _F32)],
        compiler_params=_cparams(1),
        name="from_scan",
    )(y.reshape(t * pitch, lanes))


def _wkv_scan_kernel(r_ref, w_ref, k_ref, v_ref, a_ref, b_ref, y_ref, s_ref):
    tc, pitch, lanes = r_ref.shape
    n = s_ref.shape[0]

    @pl.when(pl.program_id(0) == 0)
    def _():
        s_ref[...] = jnp.zeros(s_ref.shape, _F32)

    y_ref[:, n:, :] = jnp.zeros((tc, pitch - n, lanes), _F32)

    def tree(parts):
        return parts[0] + parts[1]

    for vb in range(n // _SCAN_ROWS):
        rows = slice(vb * _SCAN_ROWS, (vb + 1) * _SCAN_ROWS)
        sa0 = [None, None]
        for k in range(n):
            p = s_ref[k, rows, :] * a_ref[0, k:k + 1, :]
            sa0[k % 2] = p if sa0[k % 2] is None else sa0[k % 2] + p

        def step(t, sa, rows=rows):
            t_next = jnp.minimum(t + 1, tc - 1)
            vv = v_ref[t, rows, :]
            y, sa_next = [None, None], [None, None]
            for k in range(n):
                row = slice(k, k + 1)
                s = s_ref[k, rows, :] * w_ref[t, row, :] + sa * b_ref[t, row, :] + vv * k_ref[t, row, :]
                s_ref[k, rows, :] = s
                py, ps = s * r_ref[t, row, :], s * a_ref[t_next, row, :]
                y[k % 2] = py if y[k % 2] is None else y[k % 2] + py
                sa_next[k % 2] = ps if sa_next[k % 2] is None else sa_next[k % 2] + ps
            y_ref[t, rows, :] = tree(y)
            return tree(sa_next)

        lax.fori_loop(0, tc, step, tree(sa0))


def _wkv_scan(r, w, k, v, a, b, n):
    t, pitch, lanes = r.shape
    tc = _block(t, 64)
    blk = pl.BlockSpec((tc, pitch, lanes), lambda c: (c, 0, 0))
    return pl.pallas_call(
        _wkv_scan_kernel,
        grid=(t // tc,),
        in_specs=[blk] * 6,
        out_specs=[blk, pl.BlockSpec((n, n, lanes), lambda c: (0, 0, 0))],
        out_shape=[jax.ShapeDtypeStruct((t, pitch, lanes), _F32), jax.ShapeDtypeStruct((n, n, lanes), _F32)],
        compiler_params=_cparams(1),
        name="wkv_scan",
    )(r, w, k, v, a, b)


def _wkv_step_kernel(s_ref, r_ref, w_ref, k_ref, v_ref, a_ref, b_ref, eye_ref, y_ref, so_ref):
    eye = eye_ref[...]

    def head(i, carry):
        row = pl.ds(i, 1)
        s = s_ref[i]
        sa = jnp.sum(s * a_ref[row, :], axis=1, keepdims=True)
        v_col = jnp.sum(v_ref[row, :] * eye, axis=1, keepdims=True)
        s = s * w_ref[row, :] + sa * b_ref[row, :] + v_col * k_ref[row, :]
        so_ref[i] = s
        y_col = jnp.sum(s * r_ref[row, :], axis=1, keepdims=True)
        y_ref[row, :] = jnp.sum(y_col * eye, axis=0, keepdims=True)
        return carry

    lax.fori_loop(0, s_ref.shape[0], head, 0, unroll=_SUBLANE)


def _wkv_step(s, r, w, k, v, a, b):
    nh, n, _ = s.shape
    hb = _block(nh, 128)
    vec = pl.BlockSpec((hb, n), lambda i: (i, 0))
    st = pl.BlockSpec((hb, n, n), lambda i: (i, 0, 0))
    eye = jnp.eye(n, dtype=_F32)
    return pl.pallas_call(
        _wkv_step_kernel,
        grid=(nh // hb,),
        in_specs=[st] + [vec] * 6 + [pl.BlockSpec((n, n), lambda i: (0, 0))],
        out_specs=[vec, st],
        out_shape=[jax.ShapeDtypeStruct((nh, n), _F32), jax.ShapeDtypeStruct((nh, n, n), _F32)],
        compiler_params=_cparams(1),
        name="wkv_step",
    )(s, r, w, k, v, a, b, eye)


def _pad_rows(w, rows):
    return jnp.pad(w, ((0, rows - w.shape[0]), (0, 0)))


def _pad_cols(w, cols):
    return jnp.pad(w, ((0, 0), (0, cols - w.shape[1])))


def _up(n, mult):
    return -(-n // mult) * mult


def _layer_weights(norm_mix, w_in, mu_shift, w0, w_decay_up, a0, w_aaa_up, w_gate_up, k_k, k_a, r_k,
                   ln_x_g, ln_x_b, w_o_a, w_pool, pool_scale, w_o_b, w_out, norm_ffn, w_ffn_gate, w_ffn_up,
                   conv_k, conv_b, w_ffn_down, norm_ple, w_ple_gate, w_ple):
    da = w0.shape[0]
    nh, hs = r_k.shape
    lw, la, lg = w_decay_up.shape[0], w_aaa_up.shape[0], w_gate_up.shape[0]
    lwp, lap, lgp = _up(lw, _LANE), _up(la, _LANE), _up(lg, _LANE)
    a_proj = 3 * da + lw + la + lg
    db = pool_scale.shape[0]
    d = w_in.shape[0]
    o = 3 * da

    def regroup(x):
        return jnp.concatenate([
            x[:, :o], _pad_cols(x[:, o:o + lw], lwp), _pad_cols(x[:, o + lw:o + lw + la], lap),
            _pad_cols(x[:, o + lw + la:a_proj], lgp)], axis=1)

    ew = _MXU_DIM if (da % _MXU_DIM == 0 and _MXU_DIM % hs == 0) else hs
    seg = jnp.arange(ew) // hs
    return dict(
        dims=dict(da=da, nh=nh, hs=hs, lw=lw, la=la, lg=lg, lwp=lwp, lap=lap, a_proj=a_proj, db=db, d=d),
        norm_mix=norm_mix, norm_ffn=norm_ffn, norm_ple=norm_ple,
        w_a=regroup(w_in[:, :a_proj]).astype(_BF16),
        w_u=w_in[:, a_proj:a_proj + db].astype(_BF16),
        w_gates=w_in[:, a_proj + db:].astype(_BF16),
        mu=regroup(mu_shift[None, :]),
        vec_prep=_pad_rows(jnp.stack([w0, a0, k_k, k_a]), _SUBLANE),
        vec_post=_pad_rows(jnp.stack([ln_x_g, ln_x_b, r_k.reshape(-1)]), _SUBLANE),
        wd=_pad_rows(w_decay_up, lwp).astype(_BF16),
        wa=_pad_rows(w_aaa_up, lap).astype(_BF16),
        wg=_pad_rows(w_gate_up, lgp).astype(_BF16),
        seg_ones=(seg[:, None] == seg[None, :]).astype(_BF16),
        w_o_a=w_o_a.astype(_BF16), w_o_b=w_o_b.astype(_BF16), w_out=w_out.astype(_BF16),
        w_pool=w_pool.astype(_BF16), pool_scale=pool_scale[None, :],
        w_ffn_gate=w_ffn_gate.astype(_BF16), w_ffn_up=w_ffn_up.astype(_BF16), w_ffn_down=w_ffn_down.astype(_BF16),
        conv_k=_pad_rows(conv_k, _SUBLANE), conv_b=conv_b[None, :],
        w_ple_gate=w_ple_gate.astype(_BF16), w_ple=w_ple.astype(_BF16),
    )


def _ungroup(za, dm):
    o = 3 * dm["da"]
    o2 = o + dm["lwp"]
    o3 = o2 + dm["lap"]
    return jnp.concatenate([za[..., :o], za[..., o:o + dm["lw"]], za[..., o2:o2 + dm["la"]],
                            za[..., o3:o3 + dm["lg"]]], axis=-1)


def _mixer_tail(h, p, o_a, o_b, gates, lw, ffn_up):
    merged = _merge(o_a, o_b, lw["w_o_a"], lw["w_o_b"], gates)
    h = _mm_res(merged, lw["w_out"], h, "out_proj")
    hid, conv_state = ffn_up(_rmsnorm(h, lw["norm_ffn"], _BF16))
    h = _mm_res(hid, lw["w_ffn_down"], h, "ffn_down")
    h = _ple(_rmsnorm(h, lw["norm_ple"], _BF16), lw["w_ple_gate"], p.astype(_BF16), lw["w_ple"], h)
    return h, conv_state


def _layer_seq(x, p, lw):
    bsz, t, d = x.shape
    dm = lw["dims"]
    da, nh, hs = dm["da"], dm["nh"], dm["hs"]
    m = bsz * t
    h = x.reshape(m, d)
    xn = _rmsnorm(h, lw["norm_mix"], _BF16)
    za_raw = _mm(xn, lw["w_a"], bn_target=512, name="in_proj_a")
    u = _mm(xn, lw["w_u"], name="in_proj_u")
    gates = _mm(xn, lw["w_gates"], act="sigmoid", name="in_proj_gates")

    r, decay, k, v, na, nb, g = _prep_seq(za_raw, lw["mu"], lw["vec_prep"], lw["wd"], lw["wa"], lw["wg"],
                                          lw["seg_ones"], bsz, t)

    y, s_fin = _wkv_scan(*(_to_scan(z.reshape(bsz, t, da), hs) for z in (r, decay, k, v, na, nb)), hs)
    y = _from_scan(y, bsz, hs).reshape(m, da)
    new_wkv = s_fin.reshape(hs, hs, bsz, nh).transpose(2, 3, 1, 0)
    o_a = _post(y, r, k, v, g, lw["vec_post"], lw["seg_ones"], hs)
    o_b = _pool_seq(u, lw["w_pool"], lw["pool_scale"], bsz, t)

    def ffn_up(xn2):
        hid, tails = _ffn_up_seq(xn2, lw["w_ffn_gate"], lw["w_ffn_up"], lw["conv_k"], lw["conv_b"], t)
        per_seq = tails.shape[0] // bsz
        return hid, tails[per_seq - 1::per_seq]

    h, tails = _mixer_tail(h, p.reshape(m, -1), o_a, o_b, gates, lw, ffn_up)
    new_shift = _ungroup(za_raw.reshape(bsz, t, -1)[:, t - 1:, :], dm)
    n_hist = max(_POOL_WINDOWS) - 1
    new_pool = u.reshape(bsz, t, -1)[:, t - n_hist:, :]
    return h.reshape(bsz, t, d), new_wkv, new_shift, new_pool, tails


def _layer_step(x, p, s_wkv, s_shift, s_pool, s_conv, pos, lw):
    bsz, _, d = x.shape
    dm = lw["dims"]
    da, nh, hs = dm["da"], dm["nh"], dm["hs"]
    h = x.reshape(bsz, d)
    xn = _rmsnorm(h, lw["norm_mix"], _BF16)
    za_raw = _mm(xn, lw["w_a"], bn_target=512, name="in_proj_a")
    u = _mm(xn, lw["w_u"], name="in_proj_u")
    gates = _mm(xn, lw["w_gates"], act="sigmoid", name="in_proj_gates")

    o = 3 * da
    sh = s_shift.reshape(bsz, -1)
    prev = jnp.concatenate([sh[:, :o], _pad_cols(sh[:, o:o + dm["lw"]], dm["lwp"]),
                            _pad_cols(sh[:, o + dm["lw"]:o + dm["lw"] + dm["la"]], dm["lap"]),
                            _pad_cols(sh[:, o + dm["lw"] + dm["la"]:], za_raw.shape[1] - o - dm["lwp"] - dm["lap"])], axis=1)
    r, decay, k, v, na, nb, g = _prep_step(za_raw, prev, lw["mu"], lw["vec_prep"], lw["wd"], lw["wa"], lw["wg"],
                                           lw["seg_ones"])
    heads = lambda z: z.reshape(bsz * nh, hs)
    y, s_new = _wkv_step(s_wkv.reshape(bsz * nh, hs, hs), *(heads(z) for z in (r, decay, k, v, na, nb)))
    o_a = _post(y.reshape(bsz, da), r, k, v, g, lw["vec_post"], lw["seg_ones"], hs)

    ext = jnp.concatenate([s_pool, u[:, None, :]], axis=1)
    o_b = _pool_step(ext, lw["w_pool"], lw["pool_scale"], pos)

    def ffn_up(xn2):
        return _ffn_up_step(xn2, lw["w_ffn_gate"], lw["w_ffn_up"], lw["conv_k"], lw["conv_b"],
                            s_conv[:, 0, :], s_conv[:, 1, :])

    h, hg = _mixer_tail(h, p.reshape(bsz, -1), o_a, o_b, gates, lw, ffn_up)
    new_shift = _ungroup(za_raw, dm)[:, None, :]
    new_conv = jnp.concatenate([s_conv[:, 1:, :], hg[:, None, :]], axis=1)
    return h.reshape(bsz, 1, d), s_new.reshape(bsz, nh, hs, hs), new_shift, ext[:, 1:, :], new_conv


def kernel(x_prompt, x_sample, p_prompt, p_sample, state_wkv, state_shift, state_pool, state_conv, norm_mix, w_in, mu_shift, w0, w_decay_up, a0, w_aaa_up, w_gate_up, k_k, k_a, r_k, ln_x_g, ln_x_b, w_o_a, w_pool, pool_scale, w_o_b, w_out, norm_ffn, w_ffn_gate, w_ffn_up, conv_k, conv_b, w_ffn_down, norm_ple, w_ple_gate, w_ple, norm_final):
    depth = w_in.shape[0]
    conv_w = conv_k.shape[1]
    per_layer = (norm_mix, w_in, mu_shift, w0, w_decay_up, a0, w_aaa_up, w_gate_up, k_k, k_a, r_k, ln_x_g, ln_x_b,
                 w_o_a, w_pool, pool_scale, w_o_b, w_out, norm_ffn, w_ffn_gate, w_ffn_up, conv_k, conv_b,
                 w_ffn_down, norm_ple, w_ple_gate, w_ple)
    hp, hs = x_prompt, x_sample
    outs_p, outs_s = ([], [], [], []), ([], [], [], [])
    for i in range(depth):
        lw = _layer_weights(*(arr[i] for arr in per_layer))
        hp, wkv_p, shift_p, pool_p, tails_p = _layer_seq(hp, p_prompt[i], lw)
        conv_p = tails_p[:, _SUBLANE - (conv_w - 1):, :]
        hs, wkv_s, shift_s, pool_s, conv_s = _layer_step(hs, p_sample[i], state_wkv[i], state_shift[i],
                                                         state_pool[i], state_conv[i], _PAST_LEN, lw)
        for lst, s in zip(outs_p, (wkv_p, shift_p, pool_p, conv_p)):
            lst.append(s)
        for lst, s in zip(outs_s, (wkv_s, shift_s, pool_s, conv_s)):
            lst.append(s)
    y_prompt = _rmsnorm(hp.reshape(-1, hp.shape[-1]), norm_final, hp.dtype).reshape(hp.shape)
    y_sample = _rmsnorm(hs.reshape(-1, hs.shape[-1]), norm_final, hs.dtype).reshape(hs.shape)
    return (y_prompt, y_sample,
            jnp.stack(outs_p[0]), jnp.stack(outs_p[1]), jnp.stack(outs_p[2]), jnp.stack(outs_p[3]),
            jnp.stack(outs_s[0]), jnp.stack(outs_s[1]), jnp.stack(outs_s[2]), jnp.stack(outs_s[3]))
```

```python
import functools

import jax
import jax.numpy as jnp
from jax import lax
from jax.experimental import pallas as pl
from jax.experimental.pallas import tpu as pltpu

_EPS = 1e-6
_GN_EPS = 64e-5
_POOL_WINDOWS = (2, 4, 8, 16)
_PAST_LEN = 16384
_KK_FLOOR = 1e-12

_LANE = 128
_SUBLANE = 8
_MXU_DIM = 256
_VMEM_LIMIT_BYTES = 56 * 1024 * 1024

_F32 = jnp.float32
_BF16 = jnp.bfloat16


def _cparams(n_grid):
    return pltpu.CompilerParams(dimension_semantics=("arbitrary",) * n_grid,
                                vmem_limit_bytes=_VMEM_LIMIT_BYTES)


def _block(n, target):
    if n <= target:
        return n
    b = target
    while b >= _SUBLANE:
        if n % b == 0 and b % _SUBLANE == 0:
            return b
        b -= _SUBLANE
    return n


def _sigmoid(x):
    return 1.0 / (1.0 + jnp.exp(-x))


def _softplus(x):
    return jnp.maximum(x, 0.0) + jnp.log1p(jnp.exp(-jnp.abs(x)))


def _gelu_tanh(x):
    return x * (0.5 * (1.0 + jnp.tanh(0.7978845608028654 * (x + 0.044715 * (x * x * x)))))


def _dot(a, b):
    return jnp.dot(a, b, preferred_element_type=_F32)


def _rmsnorm_kernel(x_ref, g_ref, o_ref):
    x = x_ref[...]
    y = x * lax.rsqrt(jnp.mean(x * x, axis=-1, keepdims=True) + _EPS)
    o_ref[...] = (y * g_ref[...]).astype(o_ref.dtype)


def _rmsnorm(x, g, out_dtype):
    m, d = x.shape
    bm = _block(m, 256)
    return pl.pallas_call(
        _rmsnorm_kernel,
        grid=(m // bm,),
        in_specs=[pl.BlockSpec((bm, d), lambda i: (i, 0)), pl.BlockSpec((1, d), lambda i: (0, 0))],
        out_specs=pl.BlockSpec((bm, d), lambda i: (i, 0)),
        out_shape=jax.ShapeDtypeStruct((m, d), out_dtype),
        compiler_params=_cparams(1),
        name="rmsnorm",
    )(x, g.reshape(1, d))


def _mm_kernel(x_ref, w_ref, o_ref, *, act):
    acc = _dot(x_ref[...], w_ref[...])
    if act == "sigmoid":
        acc = _sigmoid(acc)
    o_ref[...] = acc.astype(o_ref.dtype)


def _mm(x, w, *, act=None, out_dtype=_F32, bn_target=1024, name="mm"):
    m, k = x.shape
    n = w.shape[1]
    bm, bn = _block(m, 1024), _block(n, bn_target)
    return pl.pallas_call(
        functools.partial(_mm_kernel, act=act),
        grid=(m // bm, n // bn),
        in_specs=[pl.BlockSpec((bm, k), lambda i, j: (i, 0)), pl.BlockSpec((k, bn), lambda i, j: (0, j))],
        out_specs=pl.BlockSpec((bm, bn), lambda i, j: (i, j)),
        out_shape=jax.ShapeDtypeStruct((m, n), out_dtype),
        compiler_params=_cparams(2),
        name=name,
    )(x, w)


def _mm_nt_kernel(x_ref, wt_ref, o_ref, wb_ref, *, act):
    @pl.when(pl.program_id(1) == 0)
    def _():
        wb_ref[...] = wt_ref[...].astype(_BF16)

    acc = lax.dot_general(x_ref[...], wb_ref[...], (((1,), (1,)), ((), ())), preferred_element_type=_F32)
    if act == "sigmoid":
        acc = _sigmoid(acc)
    o_ref[...] = acc.astype(o_ref.dtype)


def _mm_nt(x, wt, *, act=None, name="mm_nt"):
    m, k = x.shape
    n = wt.shape[0]
    bm, bn = _block(m, 1024), _block(n, 512)
    return pl.pallas_call(
        functools.partial(_mm_nt_kernel, act=act),
        grid=(n // bn, m // bm),
        in_specs=[pl.BlockSpec((bm, k), lambda j, i: (i, 0)), pl.BlockSpec((bn, k), lambda j, i: (j, 0))],
        out_specs=pl.BlockSpec((bm, bn), lambda j, i: (i, j)),
        out_shape=jax.ShapeDtypeStruct((m, n), _F32),
        scratch_shapes=[pltpu.VMEM((bn, k), _BF16)],
        compiler_params=_cparams(2),
        name=name,
    )(x, wt)


def _merge_kernel(oa_ref, ob_ref, wa_ref, wb_ref, ga_ref, gb_ref, o_ref):
    a = _dot(oa_ref[...], wa_ref[...])
    b = _dot(ob_ref[...], wb_ref[...])
    o_ref[...] = (ga_ref[...] * a + gb_ref[...] * b).astype(o_ref.dtype)


def _merge(oa, ob, wa, wb, gates):
    m, ka = oa.shape
    kb = ob.shape[1]
    n = wa.shape[1]
    bm, bn = _block(m, 1024), _block(n, 512)
    nj = n // bn
    return pl.pallas_call(
        _merge_kernel,
        grid=(m // bm, nj),
        in_specs=[
            pl.BlockSpec((bm, ka), lambda i, j: (i, 0)),
            pl.BlockSpec((bm, kb), lambda i, j: (i, 0)),
            pl.BlockSpec((ka, bn), lambda i, j: (0, j)),
            pl.BlockSpec((kb, bn), lambda i, j: (0, j)),
            pl.BlockSpec((bm, bn), lambda i, j: (i, j)),
            pl.BlockSpec((bm, bn), lambda i, j: (i, j + nj)),
        ],
        out_specs=pl.BlockSpec((bm, bn), lambda i, j: (i, j)),
        out_shape=jax.ShapeDtypeStruct((m, n), _BF16),
        compiler_params=_cparams(2),
        name="merge",
    )(oa, ob, wa, wb, gates, gates)


def _mm_res_kernel(x_ref, w_ref, r_ref, o_ref):
    o_ref[...] = r_ref[...] + _dot(x_ref[...], w_ref[...])


def _mm_res_ktiled_kernel(x_ref, w_ref, r_ref, o_ref, acc_ref):
    kk = pl.program_id(2)

    @pl.when(kk == 0)
    def _():
        acc_ref[...] = r_ref[...]

    acc_ref[...] += _dot(x_ref[...], w_ref[...])

    @pl.when(kk == pl.num_programs(2) - 1)
    def _():
        o_ref[...] = acc_ref[...]


def _mm_res(x, w, res, name):
    m, k = x.shape
    n = w.shape[1]
    bm, bn, bk = _block(m, 1024), _block(n, 512), _block(k, 4096)
    if bk == k:
        return pl.pallas_call(
            _mm_res_kernel,
            grid=(m // bm, n // bn),
            in_specs=[
                pl.BlockSpec((bm, k), lambda i, j: (i, 0)),
                pl.BlockSpec((k, bn), lambda i, j: (0, j)),
                pl.BlockSpec((bm, bn), lambda i, j: (i, j)),
            ],
            out_specs=pl.BlockSpec((bm, bn), lambda i, j: (i, j)),
            out_shape=jax.ShapeDtypeStruct((m, n), _F32),
            compiler_params=_cparams(2),
            name=name,
        )(x, w, res)
    return pl.pallas_call(
        _mm_res_ktiled_kernel,
        grid=(m // bm, n // bn, k // bk),
        in_specs=[
            pl.BlockSpec((bm, bk), lambda i, j, q: (i, q)),
            pl.BlockSpec((bk, bn), lambda i, j, q: (q, j)),
            pl.BlockSpec((bm, bn), lambda i, j, q: (i, j)),
        ],
        out_specs=pl.BlockSpec((bm, bn), lambda i, j, q: (i, j)),
        out_shape=jax.ShapeDtypeStruct((m, n), _F32),
        scratch_shapes=[pltpu.VMEM((bm, bn), _F32)],
        compiler_params=_cparams(3),
        name=name,
    )(x, w, res)


def _ple_kernel(x_ref, wg_ref, p_ref, wp_ref, r_ref, o_ref):
    gate = _sigmoid(_dot(x_ref[...], wg_ref[...]))
    o_ref[...] = r_ref[...] + gate * _dot(p_ref[...], wp_ref[...])


def _ple(xn, wg, p, wp, res):
    m, k = xn.shape
    kp = p.shape[1]
    n = wg.shape[1]
    bm, bn = _block(m, 1024), _block(n, 512)
    return pl.pallas_call(
        _ple_kernel,
        grid=(m // bm, n // bn),
        in_specs=[
            pl.BlockSpec((bm, k), lambda i, j: (i, 0)),
            pl.BlockSpec((k, bn), lambda i, j: (0, j)),
            pl.BlockSpec((bm, kp), lambda i, j: (i, 0)),
            pl.BlockSpec((kp, bn), lambda i, j: (0, j)),
            pl.BlockSpec((bm, bn), lambda i, j: (i, j)),
        ],
        out_specs=pl.BlockSpec((bm, bn), lambda i, j: (i, j)),
        out_shape=jax.ShapeDtypeStruct((m, n), _F32),
        compiler_params=_cparams(2),
        name="ple",
    )(xn, wg, p, wp, res)


def _ffn_up_seq_kernel(x_ref, wg_ref, wu_ref, ck_ref, cb_ref, h_ref, tail_ref, carry_ref, *, blocks_per_seq):
    i, j = pl.program_id(0), pl.program_id(1)

    @pl.when(i % blocks_per_seq == 0)
    def _():
        carry_ref[j] = jnp.zeros(carry_ref.shape[1:], _F32)

    hist = carry_ref[j]
    x = x_ref[...]
    hg = _dot(x, wg_ref[...])
    up = _dot(x, wu_ref[...])
    bm = hg.shape[0]
    row = lax.broadcasted_iota(jnp.int32, hg.shape, 0)
    prev1 = jnp.where(row == 0, hist[7:8], pltpu.roll(hg, 1, 0))
    prev2 = jnp.where(row == 0, hist[6:7], jnp.where(row == 1, hist[7:8], pltpu.roll(hg, 2, 0)))
    ck = ck_ref[...]
    conv = cb_ref[...] + (prev2 * ck[0:1] + prev1 * ck[1:2] + hg * ck[2:3])
    h_ref[...] = (_gelu_tanh(conv) * up).astype(h_ref.dtype)
    tail = hg[bm - _SUBLANE:, :]
    tail_ref[0] = tail
    carry_ref[j] = tail


def _ffn_up_seq(xn, wg, wu, conv_k8, conv_b, seq_len):
    m, k = xn.shape
    f = wg.shape[1]
    bm, bn = _block(seq_len, 1024), _block(f, 512)
    nj = f // bn
    return pl.pallas_call(
        functools.partial(_ffn_up_seq_kernel, blocks_per_seq=seq_len // bm),
        grid=(m // bm, nj),
        in_specs=[
            pl.BlockSpec((bm, k), lambda i, j: (i, 0)),
            pl.BlockSpec((k, bn), lambda i, j: (0, j)),
            pl.BlockSpec((k, bn), lambda i, j: (0, j)),
            pl.BlockSpec((_SUBLANE, bn), lambda i, j: (0, j)),
            pl.BlockSpec((1, bn), lambda i, j: (0, j)),
        ],
        out_specs=[
            pl.BlockSpec((bm, bn), lambda i, j: (i, j)),
            pl.BlockSpec((1, _SUBLANE, bn), lambda i, j: (i, 0, j)),
        ],
        out_shape=[
            jax.ShapeDtypeStruct((m, f), _BF16),
            jax.ShapeDtypeStruct((m // bm, _SUBLANE, f), _F32),
        ],
        scratch_shapes=[pltpu.VMEM((nj, _SUBLANE, bn), _F32)],
        compiler_params=_cparams(2),
        name="ffn_up_seq",
    )(xn, wg, wu, conv_k8, conv_b)


def _ffn_up_step_kernel(x_ref, wg_ref, wu_ref, ck_ref, cb_ref, h0_ref, h1_ref, h_ref, hg_ref):
    x = x_ref[...]
    hg = _dot(x, wg_ref[...])
    up = _dot(x, wu_ref[...])
    ck = ck_ref[...]
    c = cb_ref[...] + (h0_ref[...] * ck[0:1] + h1_ref[...] * ck[1:2] + hg * ck[2:3])
    h_ref[...] = (_gelu_tanh(c) * up).astype(h_ref.dtype)
    hg_ref[...] = hg


def _ffn_up_step(xn, wg, wu, conv_k8, conv_b, hist0, hist1):
    m, k = xn.shape
    f = wg.shape[1]
    bm, bn = _block(m, 1024), _block(f, 512)
    tile = pl.BlockSpec((bm, bn), lambda i, j: (i, j))
    return pl.pallas_call(
        _ffn_up_step_kernel,
        grid=(m // bm, f // bn),
        in_specs=[
            pl.BlockSpec((bm, k), lambda i, j: (i, 0)),
            pl.BlockSpec((k, bn), lambda i, j: (0, j)),
            pl.BlockSpec((k, bn), lambda i, j: (0, j)),
            pl.BlockSpec((_SUBLANE, bn), lambda i, j: (0, j)),
            pl.BlockSpec((1, bn), lambda i, j: (0, j)),
            tile,
            tile,
        ],
        out_specs=[tile, tile],
        out_shape=[jax.ShapeDtypeStruct((m, f), _BF16), jax.ShapeDtypeStruct((m, f), _F32)],
        compiler_params=_cparams(2),
        name="ffn_up_step",
    )(xn, wg, wu, conv_k8, conv_b, hist0, hist1)


def _pool_mix_groups(window_sum, u, wp_ref, scale, o_ref, cnt_of):
    gw = u.shape[1] // len(_POOL_WINDOWS)
    for g, win in enumerate(_POOL_WINDOWS):
        cols = slice(g * gw, (g + 1) * gw)
        pooled = window_sum(cols, win) / cnt_of(win) - u[:, cols]
        mixed = _dot(pooled.astype(_BF16), wp_ref[g]) * scale[:, cols]
        o_ref[:, cols] = mixed.astype(o_ref.dtype)


def _pool_seq_kernel(u_ref, wp_ref, sc_ref, o_ref, ext_ref, *, hist):
    tb = u_ref.shape[0]
    t0 = pl.program_id(1) * tb
    u = u_ref[...]

    @pl.when(pl.program_id(1) == 0)
    def _():
        ext_ref[0:hist, :] = jnp.zeros((hist, u.shape[1]), _F32)

    ext_ref[hist:, :] = u

    def window_sum(cols, win):
        s = ext_ref[:, cols]
        sh = 1
        while sh < win:
            s = s + pltpu.roll(s, sh, 0)
            sh *= 2
        return s[hist:, :]

    pos = t0 + lax.broadcasted_iota(jnp.int32, (tb, 1), 0)

    def cnt_of(win):
        return jnp.minimum(pos + 1, win).astype(_F32)

    _pool_mix_groups(window_sum, u, wp_ref, sc_ref[...], o_ref, cnt_of)
    ext_ref[0:hist, :] = u[tb - hist:, :]


def _pool_seq(u, w_pool, scale, batch, seq_len):
    m, c = u.shape
    hist = 2 * _SUBLANE
    tb = _block(seq_len, 256)
    nt = seq_len // tb
    g, gw, _ = w_pool.shape
    return pl.pallas_call(
        functools.partial(_pool_seq_kernel, hist=hist),
        grid=(batch, nt),
        in_specs=[
            pl.BlockSpec((tb, c), lambda b, t: (b * nt + t, 0)),
            pl.BlockSpec((g, gw, gw), lambda b, t: (0, 0, 0)),
            pl.BlockSpec((1, c), lambda b, t: (0, 0)),
        ],
        out_specs=pl.BlockSpec((tb, c), lambda b, t: (b * nt + t, 0)),
        out_shape=jax.ShapeDtypeStruct((m, c), _BF16),
        scratch_shapes=[pltpu.VMEM((hist + tb, c), _F32)],
        compiler_params=_cparams(2),
        name="pool_seq",
    )(u, w_pool, scale)


def _pool_step_kernel(ext_ref, wp_ref, sc_ref, o_ref, *, pos):
    ext = ext_ref[...]
    rows = ext.shape[1]
    u = ext[:, rows - 1, :]

    def window_sum(cols, win):
        return jnp.sum(ext[:, rows - win:, cols], axis=1)

    def cnt_of(win):
        return float(min(pos + 1, win))

    _pool_mix_groups(window_sum, u, wp_ref, sc_ref[...], o_ref, cnt_of)


def _pool_step(ext, w_pool, scale, pos):
    m, rows, c = ext.shape
    bb = _block(m, 32)
    g, gw, _ = w_pool.shape
    return pl.pallas_call(
        functools.partial(_pool_step_kernel, pos=pos),
        grid=(m // bb,),
        in_specs=[
            pl.BlockSpec((bb, rows, c), lambda i: (i, 0, 0)),
            pl.BlockSpec((g, gw, gw), lambda i: (0, 0, 0)),
            pl.BlockSpec((1, c), lambda i: (0, 0)),
        ],
        out_specs=pl.BlockSpec((bb, c), lambda i: (i, 0)),
        out_shape=jax.ShapeDtypeStruct((m, c), _BF16),
        compiler_params=_cparams(1),
        name="pool_step",
    )(ext, w_pool, scale)


def _segsum(x, e_ref):
    e = e_ref[...]
    w = e.shape[0]
    hi = x.astype(_BF16)
    lo = (x - hi.astype(_F32)).astype(_BF16)
    parts = []
    for c in range(x.shape[1] // w):
        cols = slice(c * w, (c + 1) * w)
        parts.append(_dot(hi[:, cols], e) + _dot(lo[:, cols], e))
    return jnp.concatenate(parts, axis=1)


def _rwkv_prep(za, vec_ref, wd_ref, wa_ref, wg_ref, e_ref, out_refs):
    da = wd_ref.shape[1]
    lw, la, lg = wd_ref.shape[0], wa_ref.shape[0], wg_ref.shape[0]
    vec = vec_ref[...]
    w0, a0, k_k, k_a = vec[0:1], vec[1:2], vec[2:3], vec[3:4]
    r, k, v = za[:, :da], za[:, da:2 * da], za[:, 2 * da:3 * da]
    o = 3 * da
    xw, xa, xg = za[:, o:o + lw], za[:, o + lw:o + lw + la], za[:, o + lw + la:o + lw + la + lg]
    w = -_softplus(-(w0 + _dot(jnp.tanh(xw).astype(_BF16), wd_ref[...]))) - 0.5
    decay = jnp.exp(-jnp.exp(w))
    a = _sigmoid(a0 + _dot(xa.astype(_BF16), wa_ref[...]))
    g = _dot(_sigmoid(xg).astype(_BF16), wg_ref[...])
    kk = k * k_k
    kk = kk / jnp.maximum(jnp.sqrt(_segsum(kk * kk, e_ref)), _KK_FLOOR)
    k2 = k * (1.0 + (a - 1.0) * k_a)
    for ref, val in zip(out_refs, (r, decay, k2, v, -kk, kk * a, g)):
        ref[...] = val


def _prep_seq_kernel(y_ref, mu_ref, vec_ref, wd_ref, wa_ref, wg_ref, e_ref, *rest):
    out_refs, carry_ref = rest[:7], rest[7]

    @pl.when(pl.program_id(1) == 0)
    def _():
        carry_ref[...] = jnp.zeros(carry_ref.shape, _F32)

    y = y_ref[...]
    row = lax.broadcasted_iota(jnp.int32, y.shape, 0)
    prev = jnp.where(row == 0, carry_ref[_SUBLANE - 1:_SUBLANE, :], pltpu.roll(y, 1, 0))
    carry_ref[...] = y[y.shape[0] - _SUBLANE:, :]
    za = y + mu_ref[...] * (prev - y)
    _rwkv_prep(za, vec_ref, wd_ref, wa_ref, wg_ref, e_ref, out_refs)


def _prep_step_kernel(y_ref, prev_ref, mu_ref, vec_ref, wd_ref, wa_ref, wg_ref, e_ref, *out_refs):
    y = y_ref[...]
    za = y + mu_ref[...] * (prev_ref[...] - y)
    _rwkv_prep(za, vec_ref, wd_ref, wa_ref, wg_ref, e_ref, out_refs)


def _prep_common_specs(ap, da, lw, la, lg, ew, idx):
    return [
        pl.BlockSpec((1, ap), idx),
        pl.BlockSpec((_SUBLANE, da), idx),
        pl.BlockSpec((lw, da), idx),
        pl.BlockSpec((la, da), idx),
        pl.BlockSpec((lg, da), idx),
        pl.BlockSpec((ew, ew), idx),
    ]


def _prep_seq(y, mu, vec, wd, wa, wg, e, batch, seq_len):
    m, ap = y.shape
    da = wd.shape[1]
    tb = _block(seq_len, 128)
    nt = seq_len // tb
    const = lambda b, t: (0, 0)
    rows = lambda b, t: (b * nt + t, 0)
    return pl.pallas_call(
        _prep_seq_kernel,
        grid=(batch, nt),
        in_specs=[pl.BlockSpec((tb, ap), rows)] + _prep_common_specs(ap, da, wd.shape[0], wa.shape[0], wg.shape[0], e.shape[0], const),
        out_specs=[pl.BlockSpec((tb, da), rows)] * 7,
        out_shape=[jax.ShapeDtypeStruct((m, da), _F32)] * 7,
        scratch_shapes=[pltpu.VMEM((_SUBLANE, ap), _F32)],
        compiler_params=_cparams(2),
        name="rwkv_prep_seq",
    )(y, mu, vec, wd, wa, wg, e)


def _prep_step(y, prev, mu, vec, wd, wa, wg, e):
    m, ap = y.shape
    da = wd.shape[1]
    tb = _block(m, 128)
    const = lambda i: (0, 0)
    rows = lambda i: (i, 0)
    return pl.pallas_call(
        _prep_step_kernel,
        grid=(m // tb,),
        in_specs=[pl.BlockSpec((tb, ap), rows)] * 2 + _prep_common_specs(ap, da, wd.shape[0], wa.shape[0], wg.shape[0], e.shape[0], const),
        out_specs=[pl.BlockSpec((tb, da), rows)] * 7,
        out_shape=[jax.ShapeDtypeStruct((m, da), _F32)] * 7,
        compiler_params=_cparams(1),
        name="rwkv_prep_step",
    )(y, prev, mu, vec, wd, wa, wg, e)


def _post_kernel(y_ref, r_ref, k_ref, v_ref, g_ref, vec_ref, e_ref, o_ref, *, head):
    y = y_ref[...]
    vec = vec_ref[...]
    ln_g, ln_b, r_k = vec[0:1], vec[1:2], vec[2:3]
    inv = 1.0 / head
    mu = _segsum(y, e_ref) * inv
    d = y - mu
    var = _segsum(d * d, e_ref) * inv
    yn = d * lax.rsqrt(var + _GN_EPS) * ln_g + ln_b
    v = v_ref[...]
    bonus = _segsum(r_ref[...] * k_ref[...] * r_k, e_ref) * v
    o_ref[...] = ((yn + bonus) * g_ref[...]).astype(o_ref.dtype)


def _post(y, r, k, v, g, vec, e, head):
    m, da = y.shape
    tb = _block(m, 256)
    rows = lambda i: (i, 0)
    const = lambda i: (0, 0)
    return pl.pallas_call(
        functools.partial(_post_kernel, head=head),
        grid=(m // tb,),
        in_specs=[pl.BlockSpec((tb, da), rows)] * 5 + [pl.BlockSpec((_SUBLANE, da), const), pl.BlockSpec(e.shape, const)],
        out_specs=pl.BlockSpec((tb, da), rows),
        out_shape=jax.ShapeDtypeStruct((m, da), _BF16),
        compiler_params=_cparams(1),
        name="rwkv_post",
    )(y, r, k, v, g, vec, e)


_SCAN_ROWS = 2 * _SUBLANE
_SCAN_PAD = _SUBLANE


def _to_scan_kernel(x_ref, o_ref, s_ref, *, n, pitch):
    nb, tb, c = x_ref.shape
    nh, per_tile = c // n, _LANE // n
    lanes = nb * nh
    for b in range(nb):
        for g in range(c // _LANE):
            tile = x_ref[b, :, g * _LANE:(g + 1) * _LANE].T
            for j in range(per_tile):
                bh = b * nh + g * per_tile + j
                s_ref[bh * pitch:bh * pitch + n, :] = tile[j * n:(j + 1) * n, :]
    for k in range(n):
        o_ref[pl.ds(k, tb, stride=pitch), :] = s_ref[pl.ds(k, lanes, stride=pitch), :].T
    for t in range(tb):
        o_ref[t * pitch + n:(t + 1) * pitch, :] = jnp.zeros((pitch - n, lanes), _F32)


def _to_scan(x, n):
    nb, t, c = x.shape
    lanes = nb * (c // n)
    pitch = n + _SCAN_PAD
    tb = _block(t, _LANE)
    out = pl.pallas_call(
        functools.partial(_to_scan_kernel, n=n, pitch=pitch),
        grid=(t // tb,),
        in_specs=[pl.BlockSpec((nb, tb, c), lambda i: (0, i, 0))],
        out_specs=pl.BlockSpec((tb * pitch, lanes), lambda i: (i, 0)),
        out_shape=jax.ShapeDtypeStruct((t * pitch, lanes), _F32),
        scratch_shapes=[pltpu.VMEM((lanes * pitch, tb), _F32)],
        compiler_params=_cparams(1),
        name="to_scan",
    )(x)
    return out.reshape(t, pitch, lanes)


def _from_scan_kernel(y_ref, o_ref, s_ref, *, n, pitch):
    nb, tb, c = o_ref.shape
    nh, per_tile = c // n, _LANE // n
    lanes = nb * nh
    for v in range(n):
        s_ref[pl.ds(v, lanes, stride=pitch), :] = y_ref[pl.ds(v, tb, stride=pitch), :].T
    for b in range(nb):
        for g in range(c // _LANE):
            bh = b * nh + g * per_tile
            tile = jnp.concatenate([s_ref[(bh + j) * pitch:(bh + j) * pitch + n, :] for j in range(per_tile)], axis=0)
            o_ref[b, :, g * _LANE:(g + 1) * _LANE] = tile.T


def _from_scan(y, nb, n):
    t, pitch, lanes = y.shape
    c = (lanes // nb) * n
    tb = _block(t, _LANE)
    return pl.pallas_call(
        functools.partial(_from_scan_kernel, n=n, pitch=pitch),
        grid=(t // tb,),
        in_specs=[pl.BlockSpec((tb * pitch, lanes), lambda i: (i, 0))],
        out_specs=pl.BlockSpec((nb, tb, c), lambda i: (0, i, 0)),
        out_shape=jax.ShapeDtypeStruct((nb, t, c), _F32),
        scratch_shapes=[pltpu.VMEM((lanes * pitch, tb), _F32)],
        compiler_params=_cparams(1),
        name="from_scan",
    )(y.reshape(t * pitch, lanes))


def _wkv_scan_kernel(r_ref, w_ref, k_ref, v_ref, a_ref, b_ref, y_ref, s_ref):
    tc, pitch, lanes = r_ref.shape
    n = s_ref.shape[0]

    @pl.when(pl.program_id(0) == 0)
    def _():
        s_ref[...] = jnp.zeros(s_ref.shape, _F32)

    y_ref[:, n:, :] = jnp.zeros((tc, pitch - n, lanes), _F32)

    def tree(parts):
        return parts[0] + parts[1]

    for vb in range(n // _SCAN_ROWS):
        rows = slice(vb * _SCAN_ROWS, (vb + 1) * _SCAN_ROWS)
        sa0 = [None, None]
        for k in range(n):
            p = s_ref[k, rows, :] * a_ref[0, k:k + 1, :]
            sa0[k % 2] = p if sa0[k % 2] is None else sa0[k % 2] + p

        def step(t, sa, rows=rows):
            t_next = jnp.minimum(t + 1, tc - 1)
            vv = v_ref[t, rows, :]
            y, sa_next = [None, None], [None, None]
            for k in range(n):
                row = slice(k, k + 1)
                s = s_ref[k, rows, :] * w_ref[t, row, :] + sa * b_ref[t, row, :] + vv * k_ref[t, row, :]
                s_ref[k, rows, :] = s
                py, ps = s * r_ref[t, row, :], s * a_ref[t_next, row, :]
                y[k % 2] = py if y[k % 2] is None else y[k % 2] + py
                sa_next[k % 2] = ps if sa_next[k % 2] is None else sa_next[k % 2] + ps
            y_ref[t, rows, :] = tree(y)
            return tree(sa_next)

        lax.fori_loop(0, tc, step, tree(sa0))


def _wkv_scan(r, w, k, v, a, b, n):
    t, pitch, lanes = r.shape
    tc = _block(t, 64)
    blk = pl.BlockSpec((tc, pitch, lanes), lambda c: (c, 0, 0))
    return pl.pallas_call(
        _wkv_scan_kernel,
        grid=(t // tc,),
        in_specs=[blk] * 6,
        out_specs=[blk, pl.BlockSpec((n, n, lanes), lambda c: (0, 0, 0))],
        out_shape=[jax.ShapeDtypeStruct((t, pitch, lanes), _F32), jax.ShapeDtypeStruct((n, n, lanes), _F32)],
        compiler_params=_cparams(1),
        name="wkv_scan",
    )(r, w, k, v, a, b)


def _wkv_step_kernel(s_ref, r_ref, w_ref, k_ref, v_ref, a_ref, b_ref, eye_ref, y_ref, so_ref):
    eye = eye_ref[...]

    def head(i, carry):
        row = pl.ds(i, 1)
        s = s_ref[i]
        sa = jnp.sum(s * a_ref[row, :], axis=1, keepdims=True)
        v_col = jnp.sum(v_ref[row, :] * eye, axis=1, keepdims=True)
        s = s * w_ref[row, :] + sa * b_ref[row, :] + v_col * k_ref[row, :]
        so_ref[i] = s
        y_col = jnp.sum(s * r_ref[row, :], axis=1, keepdims=True)
        y_ref[row, :] = jnp.sum(y_col * eye, axis=0, keepdims=True)
        return carry

    lax.fori_loop(0, s_ref.shape[0], head, 0, unroll=_SUBLANE)


def _wkv_step(s, r, w, k, v, a, b):
    nh, n, _ = s.shape
    hb = _block(nh, 128)
    vec = pl.BlockSpec((hb, n), lambda i: (i, 0))
    st = pl.BlockSpec((hb, n, n), lambda i: (i, 0, 0))
    eye = jnp.eye(n, dtype=_F32)
    return pl.pallas_call(
        _wkv_step_kernel,
        grid=(nh // hb,),
        in_specs=[st] + [vec] * 6 + [pl.BlockSpec((n, n), lambda i: (0, 0))],
        out_specs=[vec, st],
        out_shape=[jax.ShapeDtypeStruct((nh, n), _F32), jax.ShapeDtypeStruct((nh, n, n), _F32)],
        compiler_params=_cparams(1),
        name="wkv_step",
    )(s, r, w, k, v, a, b, eye)


def _pad_rows(w, rows):
    return jnp.pad(w, ((0, rows - w.shape[0]), (0, 0)))


def _pad_cols(w, cols):
    return jnp.pad(w, ((0, 0), (0, cols - w.shape[1])))


def _up(n, mult):
    return -(-n // mult) * mult


def _layer_weights(norm_mix, w_in, mu_shift, w0, w_decay_up, a0, w_aaa_up, w_gate_up, k_k, k_a, r_k,
                   ln_x_g, ln_x_b, w_o_a, w_pool, pool_scale, w_o_b, w_out, norm_ffn, w_ffn_gate, w_ffn_up,
                   conv_k, conv_b, w_ffn_down, norm_ple, w_ple_gate, w_ple):
    da = w0.shape[0]
    nh, hs = r_k.shape
    lw, la, lg = w_decay_up.shape[0], w_aaa_up.shape[0], w_gate_up.shape[0]
    lwp, lap, lgp = _up(lw, _LANE), _up(la, _LANE), _up(lg, _LANE)
    a_proj = 3 * da + lw + la + lg
    db = pool_scale.shape[0]
    d = w_in.shape[0]
    o = 3 * da

    def regroup(x):
        return jnp.concatenate([
            x[:, :o], _pad_cols(x[:, o:o + lw], lwp), _pad_cols(x[:, o + lw:o + lw + la], lap),
            _pad_cols(x[:, o + lw + la:a_proj], lgp)], axis=1)

    ew = _MXU_DIM if (da % _MXU_DIM == 0 and _MXU_DIM % hs == 0) else hs
    seg = jnp.arange(ew) // hs
    return dict(
        dims=dict(da=da, nh=nh, hs=hs, lw=lw, la=la, lg=lg, lwp=lwp, lap=lap, a_proj=a_proj, db=db, d=d),
        norm_mix=norm_mix, norm_ffn=norm_ffn, norm_ple=norm_ple,
        w_a=regroup(w_in[:, :a_proj]).T,
        w_u=w_in.T[a_proj:a_proj + db],
        w_gates=w_in.T[a_proj + db:],
        mu=regroup(mu_shift[None, :]),
        vec_prep=_pad_rows(jnp.stack([w0, a0, k_k, k_a]), _SUBLANE),
        vec_post=_pad_rows(jnp.stack([ln_x_g, ln_x_b, r_k.reshape(-1)]), _SUBLANE),
        wd=_pad_rows(w_decay_up, lwp).astype(_BF16),
        wa=_pad_rows(w_aaa_up, lap).astype(_BF16),
        wg=_pad_rows(w_gate_up, lgp).astype(_BF16),
        seg_ones=(seg[:, None] == seg[None, :]).astype(_BF16),
        w_o_a=w_o_a.astype(_BF16), w_o_b=w_o_b.astype(_BF16), w_out=w_out.astype(_BF16),
        w_pool=w_pool.astype(_BF16), pool_scale=pool_scale[None, :],
        w_ffn_gate=w_ffn_gate.astype(_BF16), w_ffn_up=w_ffn_up.astype(_BF16), w_ffn_down=w_ffn_down.astype(_BF16),
        conv_k=_pad_rows(conv_k, _SUBLANE), conv_b=conv_b[None, :],
        w_ple_gate=w_ple_gate.astype(_BF16), w_ple=w_ple.astype(_BF16),
    )


def _ungroup(za, dm):
    o = 3 * dm["da"]
    o2 = o + dm["lwp"]
    o3 = o2 + dm["lap"]
    return jnp.concatenate([za[..., :o], za[..., o:o + dm["lw"]], za[..., o2:o2 + dm["la"]],
                            za[..., o3:o3 + dm["lg"]]], axis=-1)


def _mixer_tail(h, p, o_a, o_b, gates, lw, ffn_up):
    merged = _merge(o_a, o_b, lw["w_o_a"], lw["w_o_b"], gates)
    h = _mm_res(merged, lw["w_out"], h, "out_proj")
    hid, conv_state = ffn_up(_rmsnorm(h, lw["norm_ffn"], _BF16))
    h = _mm_res(hid, lw["w_ffn_down"], h, "ffn_down")
    h = _ple(_rmsnorm(h, lw["norm_ple"], _BF16), lw["w_ple_gate"], p.astype(_BF16), lw["w_ple"], h)
    return h, conv_state


def _layer_seq(x, p, lw):
    bsz, t, d = x.shape
    dm = lw["dims"]
    da, nh, hs = dm["da"], dm["nh"], dm["hs"]
    m = bsz * t
    h = x.reshape(m, d)
    xn = _rmsnorm(h, lw["norm_mix"], _BF16)
    za_raw = _mm_nt(xn, lw["w_a"], name="in_proj_a")
    u = _mm_nt(xn, lw["w_u"], name="in_proj_u")
    gates = _mm_nt(xn, lw["w_gates"], act="sigmoid", name="in_proj_gates")

    r, decay, k, v, na, nb, g = _prep_seq(za_raw, lw["mu"], lw["vec_prep"], lw["wd"], lw["wa"], lw["wg"],
                                          lw["seg_ones"], bsz, t)

    y, s_fin = _wkv_scan(*(_to_scan(z.reshape(bsz, t, da), hs) for z in (r, decay, k, v, na, nb)), hs)
    y = _from_scan(y, bsz, hs).reshape(m, da)
    new_wkv = s_fin.reshape(hs, hs, bsz, nh).transpose(2, 3, 1, 0)
    o_a = _post(y, r, k, v, g, lw["vec_post"], lw["seg_ones"], hs)
    o_b = _pool_seq(u, lw["w_pool"], lw["pool_scale"], bsz, t)

    def ffn_up(xn2):
        hid, tails = _ffn_up_seq(xn2, lw["w_ffn_gate"], lw["w_ffn_up"], lw["conv_k"], lw["conv_b"], t)
        per_seq = tails.shape[0] // bsz
        return hid, tails[per_seq - 1::per_seq]

    h, tails = _mixer_tail(h, p.reshape(m, -1), o_a, o_b, gates, lw, ffn_up)
    new_shift = _ungroup(za_raw.reshape(bsz, t, -1)[:, t - 1:, :], dm)
    n_hist = max(_POOL_WINDOWS) - 1
    new_pool = u.reshape(bsz, t, -1)[:, t - n_hist:, :]
    return h.reshape(bsz, t, d), new_wkv, new_shift, new_pool, tails


def _layer_step(x, p, s_wkv, s_shift, s_pool, s_conv, pos, lw):
    bsz, _, d = x.shape
    dm = lw["dims"]
    da, nh, hs = dm["da"], dm["nh"], dm["hs"]
    h = x.reshape(bsz, d)
    xn = _rmsnorm(h, lw["norm_mix"], _BF16)
    za_raw = _mm_nt(xn, lw["w_a"], name="in_proj_a")
    u = _mm_nt(xn, lw["w_u"], name="in_proj_u")
    gates = _mm_nt(xn, lw["w_gates"], act="sigmoid", name="in_proj_gates")

    o = 3 * da
    sh = s_shift.reshape(bsz, -1)
    prev = jnp.concatenate([sh[:, :o], _pad_cols(sh[:, o:o + dm["lw"]], dm["lwp"]),
                            _pad_cols(sh[:, o + dm["lw"]:o + dm["lw"] + dm["la"]], dm["lap"]),
                            _pad_cols(sh[:, o + dm["lw"] + dm["la"]:], za_raw.shape[1] - o - dm["lwp"] - dm["lap"])], axis=1)
    r, decay, k, v, na, nb, g = _prep_step(za_raw, prev, lw["mu"], lw["vec_prep"], lw["wd"], lw["wa"], lw["wg"],
                                           lw["seg_ones"])
    heads = lambda z: z.reshape(bsz * nh, hs)
    y, s_new = _wkv_step(s_wkv.reshape(bsz * nh, hs, hs), *(heads(z) for z in (r, decay, k, v, na, nb)))
    o_a = _post(y.reshape(bsz, da), r, k, v, g, lw["vec_post"], lw["seg_ones"], hs)

    ext = jnp.concatenate([s_pool, u[:, None, :]], axis=1)
    o_b = _pool_step(ext, lw["w_pool"], lw["pool_scale"], pos)

    def ffn_up(xn2):
        return _ffn_up_step(xn2, lw["w_ffn_gate"], lw["w_ffn_up"], lw["conv_k"], lw["conv_b"],
                            s_conv[:, 0, :], s_conv[:, 1, :])

    h, hg = _mixer_tail(h, p.reshape(bsz, -1), o_a, o_b, gates, lw, ffn_up)
    new_shift = _ungroup(za_raw, dm)[:, None, :]
    new_conv = jnp.concatenate([s_conv[:, 1:, :], hg[:, None, :]], axis=1)
    return h.reshape(bsz, 1, d), s_new.reshape(bsz, nh, hs, hs), new_shift, ext[:, 1:, :], new_conv


def kernel(x_prompt, x_sample, p_prompt, p_sample, state_wkv, state_shift, state_pool, state_conv, norm_mix, w_in, mu_shift, w0, w_decay_up, a0, w_aaa_up, w_gate_up, k_k, k_a, r_k, ln_x_g, ln_x_b, w_o_a, w_pool, pool_scale, w_o_b, w_out, norm_ffn, w_ffn_gate, w_ffn_up, conv_k, conv_b, w_ffn_down, norm_ple, w_ple_gate, w_ple, norm_final):
    depth = w_in.shape[0]
    conv_w = conv_k.shape[1]
    per_layer = (norm_mix, w_in, mu_shift, w0, w_decay_up, a0, w_aaa_up, w_gate_up, k_k, k_a, r_k, ln_x_g, ln_x_b,
                 w_o_a, w_pool, pool_scale, w_o_b, w_out, norm_ffn, w_ffn_gate, w_ffn_up, conv_k, conv_b,
                 w_ffn_down, norm_ple, w_ple_gate, w_ple)
    hp, hs = x_prompt, x_sample
    outs_p, outs_s = ([], [], [], []), ([], [], [], [])
    for i in range(depth):
        lw = _layer_weights(*(arr[i] for arr in per_layer))
        hp, wkv_p, shift_p, pool_p, tails_p = _layer_seq(hp, p_prompt[i], lw)
        conv_p = tails_p[:, _SUBLANE - (conv_w - 1):, :]
        hs, wkv_s, shift_s, pool_s, conv_s = _layer_step(hs, p_sample[i], state_wkv[i], state_shift[i],
                                                         state_pool[i], state_conv[i], _PAST_LEN, lw)
        for lst, s in zip(outs_p, (wkv_p, shift_p, pool_p, conv_p)):
            lst.append(s)
        for lst, s in zip(outs_s, (wkv_s, shift_s, pool_s, conv_s)):
            lst.append(s)
    y_prompt = _rmsnorm(hp.reshape(-1, hp.shape[-1]), norm_final, hp.dtype).reshape(hp.shape)
    y_sample = _rmsnorm(hs.reshape(-1, hs.shape[-1]), norm_final, hs.dtype).reshape(hs.shape)
    return (y_prompt, y_sample,
            jnp.stack(outs_p[0]), jnp.stack(outs_p[1]), jnp.stack(outs_p[2]), jnp.stack(outs_p[3]),
            jnp.stack(outs_s[0]), jnp.stack(outs_s[1]), jnp.stack(outs_s[2]), jnp.stack(outs_s[3]))
```
